```python
import math
import jax, jax.numpy as jnp
from jax import lax
import numpy as np

D_MODEL = 4096
BATCH = 16
SEQ = 256
DEPTH = 2
DEC_BATCH = 4
DEC_SEQ = 2048
PAST_LEN = 256

GRID_W = 64
D_SSD = D_MODEL
SSD_HEAD_DIM = 64
SSD_HEADS = D_SSD // SSD_HEAD_DIM
SSD_GROUPS = 8
SSD_STATE = 128
SSD_GN = SSD_GROUPS * SSD_STATE
SSD_CONV_W = 5
SSD_CONV_CH = D_SSD + 2 * SSD_GN
SSD_CHUNK = 128
D_NA = D_MODEL // 2
NA_HEAD_DIM = 128
NA_HEADS = D_NA // NA_HEAD_DIM
NA_WIN_R = 8
NA_WIN_C = 16
NA_QB = 16
NA_KB = 32
ATTN_QBLOCK = 128
D_S5 = D_MODEL // 2
S5_CH = 16
S5_GROUPS = D_S5 // S5_CH
S5_STATE = 64
D_FF = 4 * D_MODEL
N_MOD = 6
EPS = 1e-6
NEG_INF = -1e30
IN_SPLITS = (D_MODEL, D_MODEL, D_MODEL, D_SSD, SSD_CONV_CH, 2 * SSD_HEADS, D_NA, D_NA, D_NA, D_S5)
N_IN_COLS = sum(IN_SPLITS)

kernel_name = "hybrid_ssd_natten_s5_flow_step"


def rmsnorm(x, g):
    xf = x.astype(jnp.float32)
    y = xf * lax.rsqrt(jnp.mean(xf * xf, axis=-1, keepdims=True) + EPS)
    return (y * g.astype(jnp.float32)).astype(x.dtype)


def dwconv_centred(x, w, b):
    k = w.shape[0]
    y = lax.conv_general_dilated(x, w[:, None, :], window_strides=(1,), padding=[(k // 2, k // 2)],
                                 dimension_numbers=("NWC", "WIO", "NWC"), feature_group_count=x.shape[-1])
    return y + b


def segsum(a):
    t = a.shape[-1]
    cs = jnp.cumsum(a, axis=-1)
    diff = cs[..., :, None] - cs[..., None, :]
    mask = jnp.tril(jnp.ones((t, t), dtype=bool))
    return jnp.where(mask, diff, -jnp.inf)


def ssd_chunked(x, a_dt, bm, cm, h0):
    b, L, H, P = x.shape
    nc = L // SSD_CHUNK
    R = H // SSD_GROUPS
    xc = x.reshape(b, nc, SSD_CHUNK, SSD_GROUPS, R, P)
    ac = a_dt.reshape(b, nc, SSD_CHUNK, SSD_GROUPS, R).transpose(0, 3, 4, 1, 2)
    bc = bm.reshape(b, nc, SSD_CHUNK, SSD_GROUPS, SSD_STATE)
    cc = cm.reshape(b, nc, SSD_CHUNK, SSD_GROUPS, SSD_STATE)
    a_cs = jnp.cumsum(ac, axis=-1)
    l_mat = jnp.exp(segsum(ac))
    cb = jnp.einsum("bclgn,bcsgn->bgcls", cc, bc)
    y_diag = jnp.einsum("bgcls,bgrcls,bcsgrp->bclgrp", cb, l_mat, xc)
    decay_states = jnp.exp(a_cs[..., -1:] - a_cs)
    states = jnp.einsum("bcsgn,bgrcs,bcsgrp->bcgrpn", bc, decay_states, xc)
    states = jnp.concatenate([h0.reshape(b, 1, SSD_GROUPS, R, P, SSD_STATE), states], axis=1)
    chunk_tot = jnp.pad(a_cs[..., -1], ((0, 0), (0, 0), (0, 0), (1, 0)))
    decay_chunk = jnp.exp(segsum(chunk_tot))
    new_states = jnp.einsum("bgrzc,bcgrpn->bzgrpn", decay_chunk, states)
    states, final = new_states[:, :-1], new_states[:, -1]
    y_off = jnp.einsum("bclgn,bcgrpn,bgrcl->bclgrp", cc, states, jnp.exp(a_cs))
    y = (y_diag + y_off).reshape(b, L, H, P)
    return y, final.reshape(b, H, P, SSD_STATE)


def ssd_branch(z, xbc, dt_raw, conv_w, conv_b, dt_bias, a_log, d_skip, norm_g, h0):
    b, L, _ = z.shape
    xbc = jax.nn.silu(dwconv_centred(xbc, conv_w, conv_b))
    xs = xbc[..., :D_SSD].reshape(b, L, SSD_HEADS, SSD_HEAD_DIM)
    bm = xbc[..., D_SSD:D_SSD + SSD_GN].reshape(b, L, SSD_GROUPS, SSD_STATE)
    cm = xbc[..., D_SSD + SSD_GN:].reshape(b, L, SSD_GROUPS, SSD_STATE)
    dt = jax.nn.softplus(dt_raw.reshape(b, L, 2, SSD_HEADS) + dt_bias)
    a = -jnp.exp(a_log)
    y = d_skip[:, None] * xs
    finals = []
    for d in range(2):
        dtd = dt[:, :, d]
        xd, bd, cd = xs, bm, cm
        if d == 1:
            xd, bd, cd, dtd = jnp.flip(xd, 1), jnp.flip(bd, 1), jnp.flip(cd, 1), jnp.flip(dtd, 1)
        init = jnp.zeros((b, SSD_HEADS, SSD_HEAD_DIM, SSD_STATE), xs.dtype) if h0 is None else h0[:, d]
        yd, fd = ssd_chunked(xd * dtd[..., None], dtd * a[d], bd, cd, init)
        if d == 1:
            yd = jnp.flip(yd, 1)
        y = y + yd
        finals.append(fd)
    y = y.reshape(b, L, D_SSD) * jax.nn.silu(z)
    y = rmsnorm(y.reshape(b, L, SSD_GROUPS, D_SSD // SSD_GROUPS),
                norm_g.reshape(SSD_GROUPS, D_SSD // SSD_GROUPS)).reshape(b, L, D_SSD)
    return y, jnp.stack(finals, axis=1)


def context_attn(q, k, v):
    bsz, L, H, Dh = q.shape
    nb = L // ATTN_QBLOCK
    qb = q.reshape(bsz, nb, ATTN_QBLOCK, H, Dh).transpose(1, 0, 2, 3, 4)

    def block(qi):
        s = jnp.einsum("bqhd,bkhd->bhqk", qi, k).astype(jnp.float32) * (Dh ** -0.5)
        pr = jax.nn.softmax(s, axis=-1).astype(v.dtype)
        return jnp.einsum("bhqk,bkhd->bqhd", pr, v)

    o = lax.map(block, qb)
    return o.transpose(1, 0, 2, 3, 4).reshape(bsz, L, H * Dh)


def neighbourhood_attn(q, k, v, k_ctx, v_ctx, rpb):
    bsz, L, H, Dh = q.shape
    rows = L // GRID_W
    wr = min(NA_WIN_R, rows)
    nqb = GRID_W // NA_QB
    scale = Dh ** -0.5
    qg = q.reshape(bsz, rows, GRID_W, H, Dh)
    kg = k.reshape(bsz, rows, GRID_W, H, Dh)
    vg = v.reshape(bsz, rows, GRID_W, H, Dh)
    qcols = jnp.arange(GRID_W).reshape(nqb, NA_QB)
    kc0 = jnp.clip(jnp.arange(nqb) * NA_QB - NA_WIN_C // 2, 0, GRID_W - NA_KB)
    kcols = kc0[:, None] + jnp.arange(NA_KB)
    ws = jnp.clip(qcols - NA_WIN_C // 2, 0, GRID_W - NA_WIN_C)
    kc = kcols[:, None, :]
    col_mask = (kc >= ws[..., None]) & (kc < ws[..., None] + NA_WIN_C)
    dc_idx = jnp.clip(kc - qcols[..., None] + NA_WIN_C - 1, 0, 2 * NA_WIN_C - 2)
    rpb_cols = rpb[:, :, dc_idx]
    n_loc = wr * NA_KB

    def row_block(r):
        rs = jnp.clip(r - wr // 2, 0, rows - wr)
        k_blk = lax.dynamic_slice_in_dim(kg, rs, wr, axis=1)[:, :, kcols]
        v_blk = lax.dynamic_slice_in_dim(vg, rs, wr, axis=1)[:, :, kcols]
        q_r = lax.dynamic_index_in_dim(qg, r, axis=1, keepdims=False).reshape(bsz, nqb, NA_QB, H, Dh)
        s_loc = jnp.einsum("bjqhd,bwjkhd->bhjqwk", q_r, k_blk).astype(jnp.float32) * scale
        dr_idx = rs + jnp.arange(wr) - r + NA_WIN_R - 1
        bias = rpb_cols[:, dr_idx].transpose(0, 2, 3, 1, 4).astype(jnp.float32)
        s_loc = jnp.where(col_mask[:, :, None, :], s_loc + bias, NEG_INF)
        s_ctx = jnp.einsum("bjqhd,bchd->bhjqc", q_r, k_ctx).astype(jnp.float32) * scale
        s = jnp.concatenate([s_loc.reshape(bsz, H, nqb, NA_QB, n_loc), s_ctx], axis=-1)
        pr = jax.nn.softmax(s, axis=-1).astype(v.dtype)
        p_loc = pr[..., :n_loc].reshape(bsz, H, nqb, NA_QB, wr, NA_KB)
        o = (jnp.einsum("bhjqwk,bwjkhd->bjqhd", p_loc, v_blk)
             + jnp.einsum("bhjqc,bchd->bjqhd", pr[..., n_loc:], v_ctx))
        return o.reshape(bsz, GRID_W, H, Dh)

    o = lax.map(row_block, jnp.arange(rows))
    return o.transpose(1, 0, 2, 3, 4).reshape(bsz, L, H * Dh)


def _complex_affine_combine(e1, e2):
    a1r, a1i, b1r, b1i = e1
    a2r, a2i, b2r, b2i = e2
    return (a2r * a1r - a2i * a1i, a2r * a1i + a2i * a1r,
            a2r * b1r - a2i * b1i + b2r, a2r * b1i + a2i * b1r + b2i)


def s5_branch(u, lam_re, lam_im, log_step, b_re, b_im, c_re, c_im, d_skip, glu_w, glu_b, h0):
    bsz, L, _ = u.shape
    ug = u.reshape(bsz, L, S5_GROUPS, S5_CH)
    y = d_skip * u
    finals = []
    for d in range(2):
        step = jnp.exp(log_step[d])[:, None]
        lr, li = lam_re[d], lam_im[d]
        mag = jnp.exp(lr * step)
        ab_re, ab_im = mag * jnp.cos(li * step), mag * jnp.sin(li * step)
        den = lr * lr + li * li
        f_re = ((ab_re - 1) * lr + ab_im * li) / den
        f_im = (ab_im * lr - (ab_re - 1) * li) / den
        bb_re = f_re[..., None] * b_re - f_im[..., None] * b_im
        bb_im = f_re[..., None] * b_im + f_im[..., None] * b_re
        ud = ug if d == 0 else jnp.flip(ug, axis=1)
        bu_re = jnp.einsum("blgc,gpc->blgp", ud, bb_re)
        bu_im = jnp.einsum("blgc,gpc->blgp", ud, bb_im)
        if h0 is not None:
            h_re0, h_im0 = h0[:, d, 0], h0[:, d, 1]
            bu_re = bu_re.at[:, 0].add(ab_re * h_re0 - ab_im * h_im0)
            bu_im = bu_im.at[:, 0].add(ab_re * h_im0 + ab_im * h_re0)
        a_re = jnp.broadcast_to(ab_re, (1, L) + ab_re.shape)
        a_im = jnp.broadcast_to(ab_im, (1, L) + ab_im.shape)
        _, _, h_re, h_im = lax.associative_scan(_complex_affine_combine, (a_re, a_im, bu_re, bu_im), axis=1)
        yd = jnp.einsum("blgp,gcp->blgc", h_re, c_re[d]) - jnp.einsum("blgp,gcp->blgc", h_im, c_im[d])
        if d == 1:
            yd = jnp.flip(yd, axis=1)
        y = y + yd.reshape(bsz, L, D_S5)
        finals.append(jnp.stack([h_re[:, -1], h_im[:, -1]], axis=1))
    y = jax.nn.gelu(y)
    y = y * jax.nn.sigmoid(y @ glu_w + glu_b)
    return y, jnp.stack(finals, axis=1)


def trunk_layer(x, mod, p, ctx):
    b, L, _ = x.shape
    sh1, sc1, gt1, sh2, sc2, gt2 = jnp.split(mod, N_MOD, axis=-1)
    h = rmsnorm(x, p["norm1_g"]) * (1 + sc1) + sh1
    proj = h @ p["w_in"]
    cuts = [int(i) for i in np.cumsum(IN_SPLITS)[:-1]]
    g_ssd, g_na, g_s5, z, xbc, dt_raw, q, k, v, u = jnp.split(proj, cuts, axis=-1)
    if ctx is None:
        k_ctx = v_ctx = h0_ssd = h0_s5 = None
    else:
        k_ctx, v_ctx, h0_ssd, h0_s5 = ctx
    o_ssd, st_ssd = ssd_branch(z, xbc, dt_raw, p["ssd_conv_w"], p["ssd_conv_b"], p["ssd_dt_bias"],
                               p["ssd_a_log"], p["ssd_d"], p["ssd_norm_g"], h0_ssd)
    q = rmsnorm(q.reshape(b, L, NA_HEADS, NA_HEAD_DIM), p["na_q_g"])
    k = rmsnorm(k.reshape(b, L, NA_HEADS, NA_HEAD_DIM), p["na_k_g"])
    v = v.reshape(b, L, NA_HEADS, NA_HEAD_DIM)
    if ctx is None:
        o_na = context_attn(q, k, v)
    else:
        o_na = neighbourhood_attn(q, k, v, k_ctx, v_ctx, p["na_rpb"])
    o_s5, st_s5 = s5_branch(u, p["s5_lam_re"], p["s5_lam_im"], p["s5_log_step"], p["s5_b_re"], p["s5_b_im"],
                            p["s5_c_re"], p["s5_c_im"], p["s5_d"], p["s5_glu_w"], p["s5_glu_b"], h0_s5)
    merged = (jax.nn.sigmoid(g_ssd) * (o_ssd @ p["w_po_ssd"])
              + jax.nn.sigmoid(g_na) * (o_na @ p["w_po_na"])
              + jax.nn.sigmoid(g_s5) * (o_s5 @ p["w_po_s5"]))
    x = x + gt1 * (merged @ p["w_o"])
    h2 = rmsnorm(x, p["norm2_g"]) * (1 + sc2) + sh2
    x = x + gt2 * (jnp.square(jax.nn.relu(h2 @ p["mlp_w1"])) @ p["mlp_w2"])
    return x, (k, v, st_ssd, st_s5)


def setup_inputs(seed: int = 0) -> dict:
    key = jax.random.key(seed)
    ks = iter(jax.random.split(key, 48))
    f32 = jnp.float32

    def nrm(shape, scale=1.0):
        return jax.random.normal(next(ks), shape, f32) * scale

    def unif(shape, lo, hi):
        return jax.random.uniform(next(ks), shape, f32, lo, hi)

    dt0 = jnp.exp(unif((DEPTH, 2, SSD_HEADS), math.log(1e-3), math.log(1e-1)))
    lam_im = jnp.pi * jnp.arange(S5_STATE, dtype=f32) + nrm((DEPTH, 2, S5_GROUPS, S5_STATE), 0.01)
    return {
        "x_prompt": nrm((BATCH, SEQ, D_MODEL)),
        "x_sample": nrm((DEC_BATCH, DEC_SEQ, D_MODEL)),
        "c": nrm((DEC_BATCH, D_MODEL)),
        "cache_k": nrm((DEC_BATCH, DEPTH, PAST_LEN, NA_HEADS, NA_HEAD_DIM)),
        "cache_v": nrm((DEC_BATCH, DEPTH, PAST_LEN, NA_HEADS, NA_HEAD_DIM)),
        "state_ssd": nrm((DEC_BATCH, DEPTH, 2, SSD_HEADS, SSD_HEAD_DIM, SSD_STATE), 0.1),
        "state_s5": nrm((DEC_BATCH, DEPTH, 2, 2, S5_GROUPS, S5_STATE), 0.1),
        "c_ctx": nrm((D_MODEL,)),
        "ada_w": nrm((DEPTH, D_MODEL, N_MOD * D_MODEL), 0.5 * D_MODEL ** -0.5),
        "ada_b": nrm((DEPTH, N_MOD * D_MODEL), 0.01),
        "norm1_g": 1.0 + nrm((DEPTH, D_MODEL), 0.01),
        "norm2_g": 1.0 + nrm((DEPTH, D_MODEL), 0.01),
        "w_in": nrm((DEPTH, D_MODEL, N_IN_COLS), D_MODEL ** -0.5),
        "ssd_conv_w": nrm((DEPTH, SSD_CONV_W, SSD_CONV_CH), SSD_CONV_W ** -0.5),
        "ssd_conv_b": nrm((DEPTH, SSD_CONV_CH), 0.01),
        "ssd_dt_bias": dt0 + jnp.log(-jnp.expm1(-dt0)),
        "ssd_a_log": jnp.log(unif((DEPTH, 2, SSD_HEADS), 1.0, 16.0)),
        "ssd_d": 1.0 + nrm((DEPTH, SSD_HEADS), 0.01),
        "ssd_norm_g": 1.0 + nrm((DEPTH, D_SSD), 0.01),
        "na_q_g": 1.0 + nrm((DEPTH, NA_HEAD_DIM), 0.01),
        "na_k_g": 1.0 + nrm((DEPTH, NA_HEAD_DIM), 0.01),
        "na_rpb": nrm((DEPTH, NA_HEADS, 2 * NA_WIN_R - 1, 2 * NA_WIN_C - 1), 0.1),
        "s5_lam_re": -0.5 + nrm((DEPTH, 2, S5_GROUPS, S5_STATE), 0.01),
        "s5_lam_im": lam_im,
        "s5_log_step": unif((DEPTH, 2, S5_GROUPS), math.log(1e-3), math.log(1e-1)),
        "s5_b_re": nrm((DEPTH, S5_GROUPS, S5_STATE, S5_CH), (2 * S5_CH) ** -0.5),
        "s5_b_im": nrm((DEPTH, S5_GROUPS, S5_STATE, S5_CH), (2 * S5_CH) ** -0.5),
        "s5_c_re": nrm((DEPTH, 2, S5_GROUPS, S5_CH, S5_STATE), S5_STATE ** -0.5),
        "s5_c_im": nrm((DEPTH, 2, S5_GROUPS, S5_CH, S5_STATE), S5_STATE ** -0.5),
        "s5_d": nrm((DEPTH, D_S5)),
        "s5_glu_w": nrm((DEPTH, D_S5, D_S5), D_S5 ** -0.5),
        "s5_glu_b": nrm((DEPTH, D_S5), 0.01),
        "w_po_ssd": nrm((DEPTH, D_SSD, D_MODEL), D_SSD ** -0.5),
        "w_po_na": nrm((DEPTH, D_NA, D_MODEL), D_NA ** -0.5),
        "w_po_s5": nrm((DEPTH, D_S5, D_MODEL), D_S5 ** -0.5),
        "w_o": nrm((DEPTH, D_MODEL, D_MODEL), D_MODEL ** -0.5),
        "mlp_w1": nrm((DEPTH, D_MODEL, D_FF), D_MODEL ** -0.5),
        "mlp_w2": nrm((DEPTH, D_FF, D_MODEL), D_FF ** -0.5),
    }


def reference(x_prompt, x_sample, c, cache_k, cache_v, state_ssd, state_s5, c_ctx,
              ada_w, ada_b, norm1_g, norm2_g, w_in, ssd_conv_w, ssd_conv_b, ssd_dt_bias, ssd_a_log,
              ssd_d, ssd_norm_g, na_q_g, na_k_g, na_rpb, s5_lam_re, s5_lam_im, s5_log_step,
              s5_b_re, s5_b_im, s5_c_re, s5_c_im, s5_d, s5_glu_w, s5_glu_b,
              w_po_ssd, w_po_na, w_po_s5, w_o, mlp_w1, mlp_w2):
    y_prompt, y_sample = x_prompt, x_sample
    ks, vs, ssds, s5s = [], [], [], []
    for l in range(DEPTH):
        p = {
            "norm1_g": norm1_g[l], "norm2_g": norm2_g[l], "w_in": w_in[l],
            "ssd_conv_w": ssd_conv_w[l], "ssd_conv_b": ssd_conv_b[l], "ssd_dt_bias": ssd_dt_bias[l],
            "ssd_a_log": ssd_a_log[l], "ssd_d": ssd_d[l], "ssd_norm_g": ssd_norm_g[l],
            "na_q_g": na_q_g[l], "na_k_g": na_k_g[l], "na_rpb": na_rpb[l],
            "s5_lam_re": s5_lam_re[l], "s5_lam_im": s5_lam_im[l], "s5_log_step": s5_log_step[l],
            "s5_b_re": s5_b_re[l], "s5_b_im": s5_b_im[l], "s5_c_re": s5_c_re[l], "s5_c_im": s5_c_im[l],
            "s5_d": s5_d[l], "s5_glu_w": s5_glu_w[l], "s5_glu_b": s5_glu_b[l],
            "w_po_ssd": w_po_ssd[l], "w_po_na": w_po_na[l], "w_po_s5": w_po_s5[l], "w_o": w_o[l],
            "mlp_w1": mlp_w1[l], "mlp_w2": mlp_w2[l],
        }
        mod_ctx = (jax.nn.silu(c_ctx) @ ada_w[l] + ada_b[l])[None, None, :]
        y_prompt, (k_l, v_l, ssd_l, s5_l) = trunk_layer(y_prompt, mod_ctx, p, None)
        ks.append(k_l)
        vs.append(v_l)
        ssds.append(ssd_l)
        s5s.append(s5_l)
        mod_lat = (jax.nn.silu(c) @ ada_w[l] + ada_b[l])[:, None, :]
        ctx = (cache_k[:, l], cache_v[:, l], state_ssd[:, l], state_s5[:, l])
        y_sample, _ = trunk_layer(y_sample, mod_lat, p, ctx)
    new_cache_k = jnp.stack(ks, axis=1)
    new_cache_v = jnp.stack(vs, axis=1)
    new_state_ssd = jnp.stack(ssds, axis=1)
    new_state_s5 = jnp.stack(s5s, axis=1)
    return (y_prompt, y_sample, new_cache_k, new_cache_v, new_state_ssd, new_state_s5)
```

```python
import functools
import math

import jax
import jax.numpy as jnp
import numpy as np
from jax import lax
from jax.experimental import pallas as pl
from jax.experimental.pallas import tpu as pltpu

F32 = jnp.float32
BF16 = jnp.bfloat16

D_MODEL = 4096
BATCH = 16
SEQ = 256
DEPTH = 2
DEC_BATCH = 4
DEC_SEQ = 2048
PAST_LEN = 256
GRID_W = 64
D_SSD = D_MODEL
SSD_HEAD_DIM = 64
SSD_HEADS = D_SSD // SSD_HEAD_DIM
SSD_GROUPS = 8
SSD_STATE = 128
SSD_GN = SSD_GROUPS * SSD_STATE
SSD_CONV_W = 5
SSD_CONV_CH = D_SSD + 2 * SSD_GN
SSD_CHUNK = 128
D_NA = D_MODEL // 2
NA_HEAD_DIM = 128
NA_HEADS = D_NA // NA_HEAD_DIM
NA_WIN_R = 8
NA_WIN_C = 16
NA_QB = 16
NA_KB = 32
ATTN_QBLOCK = 128
D_S5 = D_MODEL // 2
S5_CH = 16
S5_GROUPS = D_S5 // S5_CH
S5_STATE = 64
D_FF = 4 * D_MODEL
N_MOD = 6
EPS = 1e-6
NEG_INF = -1e30

M_CTX = BATCH * SEQ
M_LAT = DEC_BATCH * DEC_SEQ
M_ALL = M_CTX + M_LAT
MOD_ROWS = 8

VMEM_LIMIT = 56 * 1024 * 1024


def _mod_row(i, tm):
    n_ctx = M_CTX // tm
    per_seq = DEC_SEQ // tm
    return jnp.where(i < n_ctx, 0, 1 + (i - n_ctx) // per_seq)


def _ada_kernel(c_ref, w_ref, b_ref, o_ref):
    c = c_ref[...]
    s = (c * jax.nn.sigmoid(c)).astype(BF16)
    acc = jnp.dot(s, w_ref[0].astype(BF16), preferred_element_type=F32)
    o_ref[0] = acc + b_ref[0]


def _ada_mod(cvec, ada_w, ada_b):
    n = N_MOD * D_MODEL
    tn = 512
    return pl.pallas_call(
        _ada_kernel,
        grid=(DEPTH, n // tn),
        in_specs=[pl.BlockSpec((MOD_ROWS, D_MODEL), lambda l, j: (0, 0)),
                  pl.BlockSpec((1, D_MODEL, tn), lambda l, j: (l, 0, j)),
                  pl.BlockSpec((1, 1, tn), lambda l, j: (l, 0, j))],
        out_specs=pl.BlockSpec((1, MOD_ROWS, tn), lambda l, j: (l, 0, j)),
        out_shape=jax.ShapeDtypeStruct((DEPTH, MOD_ROWS, n), F32),
        compiler_params=pltpu.CompilerParams(dimension_semantics=("parallel", "parallel"),
                                             vmem_limit_bytes=VMEM_LIMIT),
        name="ada_mod",
    )(cvec, ada_w, ada_b.reshape(DEPTH, 1, n))


def _norm_mod_kernel(x_ref, g_ref, sc_ref, sh_ref, o_ref):
    x = x_ref[...]
    ms = jnp.mean(x * x, axis=-1, keepdims=True)
    y = x * lax.rsqrt(ms + EPS) * g_ref[...]
    o_ref[...] = (y * (1.0 + sc_ref[0]) + sh_ref[0]).astype(o_ref.dtype)


def _norm_mod(x, g, mod, shift_chunk, scale_chunk):
    tm = 256
    return pl.pallas_call(
        _norm_mod_kernel,
        grid=(M_ALL // tm,),
        in_specs=[pl.BlockSpec((tm, D_MODEL), lambda i: (i, 0)),
                  pl.BlockSpec((1, D_MODEL), lambda i: (0, 0)),
                  pl.BlockSpec((1, 1, D_MODEL), lambda i: (_mod_row(i, tm), 0, scale_chunk)),
                  pl.BlockSpec((1, 1, D_MODEL), lambda i: (_mod_row(i, tm), 0, shift_chunk))],
        out_specs=pl.BlockSpec((tm, D_MODEL), lambda i: (i, 0)),
        out_shape=jax.ShapeDtypeStruct((M_ALL, D_MODEL), BF16),
        compiler_params=pltpu.CompilerParams(dimension_semantics=("parallel",),
                                             vmem_limit_bytes=VMEM_LIMIT),
        name="norm_mod",
    )(x, g.reshape(1, D_MODEL), mod, mod)


def _mm_kernel(*refs, nk, act, has_res):
    if has_res:
        x_ref, w_ref, res_ref, gt_ref, o_ref = refs[:5]
        scratch = refs[5:]
    else:
        x_ref, w_ref, o_ref = refs[:3]
        scratch = refs[3:]

    def finish(v):
        if act == "relu2":
            r = jnp.maximum(v, 0.0)
            v = r * r
        if has_res:
            v = res_ref[...] + gt_ref[0] * v
        o_ref[...] = v.astype(o_ref.dtype)

    part = jnp.dot(x_ref[...], w_ref[...], preferred_element_type=F32)
    if nk == 1:
        finish(part)
    else:
        acc_ref, = scratch
        k = pl.program_id(2)

        @pl.when(k == 0)
        def _():
            acc_ref[...] = part

        @pl.when(k > 0)
        def _():
            acc_ref[...] += part

        @pl.when(k == nk - 1)
        def _():
            finish(acc_ref[...])


def _matmul(x, w, *, tm, tn, tk=None, out_dtype, act=None, res=None, mod=None, gate_chunk=None, name):
    m, kdim = x.shape
    n = w.shape[1]
    tk = kdim if tk is None else tk
    nk = kdim // tk
    has_res = res is not None
    in_specs = [pl.BlockSpec((tm, tk), lambda i, j, k: (i, k)),
                pl.BlockSpec((tk, tn), lambda i, j, k: (k, j))]
    args = [x, w]
    if has_res:
        nj = n // tn
        in_specs += [pl.BlockSpec((tm, tn), lambda i, j, k: (i, j)),
                     pl.BlockSpec((1, 1, tn), lambda i, j, k: (_mod_row(i, tm), 0, gate_chunk * nj + j))]
        args += [res, mod]
    return pl.pallas_call(
        functools.partial(_mm_kernel, nk=nk, act=act, has_res=has_res),
        grid=(m // tm, n // tn, nk),
        in_specs=in_specs,
        out_specs=pl.BlockSpec((tm, tn), lambda i, j, k: (i, j)),
        out_shape=jax.ShapeDtypeStruct((m, n), out_dtype),
        scratch_shapes=[pltpu.VMEM((tm, tn), F32)] if nk > 1 else [],
        compiler_params=pltpu.CompilerParams(dimension_semantics=("parallel", "parallel", "arbitrary"),
                                             vmem_limit_bytes=VMEM_LIMIT),
        name=name,
    )(*args)


def _merge_kernel(o1_ref, o2_ref, o3_ref, w1_ref, w2_ref, w3_ref, g1_ref, g2_ref, g3_ref, out_ref):
    acc = jax.nn.sigmoid(g1_ref[...]) * jnp.dot(o1_ref[...], w1_ref[...], preferred_element_type=F32)
    acc += jax.nn.sigmoid(g2_ref[...]) * jnp.dot(o2_ref[...], w2_ref[...], preferred_element_type=F32)
    acc += jax.nn.sigmoid(g3_ref[...]) * jnp.dot(o3_ref[...], w3_ref[...], preferred_element_type=F32)
    out_ref[...] = acc.astype(out_ref.dtype)


def _merge(o_ssd, o_na, o_s5, w1, w2, w3, gates):
    tm, tn = 512, 512
    nj = D_MODEL // tn
    return pl.pallas_call(
        _merge_kernel,
        grid=(M_ALL // tm, nj),
        in_specs=[pl.BlockSpec((tm, D_SSD), lambda i, j: (i, 0)),
                  pl.BlockSpec((tm, D_NA), lambda i, j: (i, 0)),
                  pl.BlockSpec((tm, D_S5), lambda i, j: (i, 0)),
                  pl.BlockSpec((D_SSD, tn), lambda i, j: (0, j)),
                  pl.BlockSpec((D_NA, tn), lambda i, j: (0, j)),
                  pl.BlockSpec((D_S5, tn), lambda i, j: (0, j)),
                  pl.BlockSpec((tm, tn), lambda i, j: (i, j)),
                  pl.BlockSpec((tm, tn), lambda i, j: (i, nj + j)),
                  pl.BlockSpec((tm, tn), lambda i, j: (i, 2 * nj + j))],
        out_specs=pl.BlockSpec((tm, tn), lambda i, j: (i, j)),
        out_shape=jax.ShapeDtypeStruct((M_ALL, D_MODEL), BF16),
        compiler_params=pltpu.CompilerParams(dimension_semantics=("parallel", "parallel"),
                                             vmem_limit_bytes=VMEM_LIMIT),
        name="merge",
    )(o_ssd, o_na, o_s5, w1, w2, w3, gates, gates, gates)


def _rmsnorm(x, g):
    y = x * lax.rsqrt(jnp.mean(x * x, axis=-1, keepdims=True) + EPS)
    return y * g


def _dwconv_centred(x, w, b):
    k = w.shape[0]
    y = lax.conv_general_dilated(x, w[:, None, :], window_strides=(1,), padding=[(k // 2, k // 2)],
                                 dimension_numbers=("NWC", "WIO", "NWC"), feature_group_count=x.shape[-1])
    return y + b


def _segsum(a):
    t = a.shape[-1]
    cs = jnp.cumsum(a, axis=-1)
    diff = cs[..., :, None] - cs[..., None, :]
    mask = jnp.tril(jnp.ones((t, t), dtype=bool))
    return jnp.where(mask, diff, -jnp.inf)


def _ssd_chunked(x, a_dt, bm, cm, h0):
    b, L, H, P = x.shape
    nc = L // SSD_CHUNK
    R = H // SSD_GROUPS
    xc = x.reshape(b, nc, SSD_CHUNK, SSD_GROUPS, R, P)
    ac = a_dt.reshape(b, nc, SSD_CHUNK, SSD_GROUPS, R).transpose(0, 3, 4, 1, 2)
    bc = bm.reshape(b, nc, SSD_CHUNK, SSD_GROUPS, SSD_STATE)
    cc = cm.reshape(b, nc, SSD_CHUNK, SSD_GROUPS, SSD_STATE)
    a_cs = jnp.cumsum(ac, axis=-1)
    l_mat = jnp.exp(_segsum(ac))
    cb = jnp.einsum("bclgn,bcsgn->bgcls", cc, bc)
    y_diag = jnp.einsum("bgcls,bgrcls,bcsgrp->bclgrp", cb, l_mat, xc)
    decay_states = jnp.exp(a_cs[..., -1:] - a_cs)
    states = jnp.einsum("bcsgn,bgrcs,bcsgrp->bcgrpn", bc, decay_states, xc)
    states = jnp.concatenate([h0.reshape(b, 1, SSD_GROUPS, R, P, SSD_STATE), states], axis=1)
    chunk_tot = jnp.pad(a_cs[..., -1], ((0, 0), (0, 0), (0, 0), (1, 0)))
    decay_chunk = jnp.exp(_segsum(chunk_tot))
    new_states = jnp.einsum("bgrzc,bcgrpn->bzgrpn", decay_chunk, states)
    states, final = new_states[:, :-1], new_states[:, -1]
    y_off = jnp.einsum("bclgn,bcgrpn,bgrcl->bclgrp", cc, states, jnp.exp(a_cs))
    y = (y_diag + y_off).reshape(b, L, H, P)
    return y, final.reshape(b, H, P, SSD_STATE)


def _ssd_branch(z, xbc, dt_raw, conv_w, conv_b, dt_bias, a_log, d_skip, norm_g, h0):
    b, L, _ = z.shape
    xbc = jax.nn.silu(_dwconv_centred(xbc, conv_w, conv_b))
    xs = xbc[..., :D_SSD].reshape(b, L, SSD_HEADS, SSD_HEAD_DIM)
    bm = xbc[..., D_SSD:D_SSD + SSD_GN].reshape(b, L, SSD_GROUPS, SSD_STATE)
    cm = xbc[..., D_SSD + SSD_GN:].reshape(b, L, SSD_GROUPS, SSD_STATE)
    dt = jax.nn.softplus(dt_raw.reshape(b, L, 2, SSD_HEADS) + dt_bias)
    a = -jnp.exp(a_log)
    y = d_skip[:, None] * xs
    finals = []
    for d in range(2):
        dtd = dt[:, :, d]
        xd, bd, cd = xs, bm, cm
        if d == 1:
            xd, bd, cd, dtd = jnp.flip(xd, 1), jnp.flip(bd, 1), jnp.flip(cd, 1), jnp.flip(dtd, 1)
        init = jnp.zeros((b, SSD_HEADS, SSD_HEAD_DIM, SSD_STATE), xs.dtype) if h0 is None else h0[:, d]
        yd, fd = _ssd_chunked(xd * dtd[..., None], dtd * a[d], bd, cd, init)
        if d == 1:
            yd = jnp.flip(yd, 1)
        y = y + yd
        finals.append(fd)
    y = y.reshape(b, L, D_SSD) * jax.nn.silu(z)
    y = _rmsnorm(y.reshape(b, L, SSD_GROUPS, D_SSD // SSD_GROUPS),
                 norm_g.reshape(SSD_GROUPS, D_SSD // SSD_GROUPS)).reshape(b, L, D_SSD)
    return y, jnp.stack(finals, axis=1)


def _context_attn(q, k, v):
    bsz, L, H, Dh = q.shape
    s = jnp.einsum("bqhd,bkhd->bhqk", q, k).astype(F32) * (Dh ** -0.5)
    pr = jax.nn.softmax(s, axis=-1)
    return jnp.einsum("bhqk,bkhd->bqhd", pr, v).reshape(bsz, L, H * Dh)


def _neighbourhood_attn(q, k, v, k_ctx, v_ctx, rpb):
    bsz, L, H, Dh = q.shape
    rows = L // GRID_W
    wr = min(NA_WIN_R, rows)
    nqb = GRID_W // NA_QB
    scale = Dh ** -0.5
    qg = q.reshape(bsz, rows, GRID_W, H, Dh)
    kg = k.reshape(bsz, rows, GRID_W, H, Dh)
    vg = v.reshape(bsz, rows, GRID_W, H, Dh)
    qcols = jnp.arange(GRID_W).reshape(nqb, NA_QB)
    kc0 = jnp.clip(jnp.arange(nqb) * NA_QB - NA_WIN_C // 2, 0, GRID_W - NA_KB)
    kcols = kc0[:, None] + jnp.arange(NA_KB)
    ws = jnp.clip(qcols - NA_WIN_C // 2, 0, GRID_W - NA_WIN_C)
    kc = kcols[:, None, :]
    col_mask = (kc >= ws[..., None]) & (kc < ws[..., None] + NA_WIN_C)
    dc_idx = jnp.clip(kc - qcols[..., None] + NA_WIN_C - 1, 0, 2 * NA_WIN_C - 2)
    rpb_cols = rpb[:, :, dc_idx]
    n_loc = wr * NA_KB

    def row_block(r):
        rs = jnp.clip(r - wr // 2, 0, rows - wr)
        k_blk = lax.dynamic_slice_in_dim(kg, rs, wr, axis=1)[:, :, kcols]
        v_blk = lax.dynamic_slice_in_dim(vg, rs, wr, axis=1)[:, :, kcols]
        q_r = lax.dynamic_index_in_dim(qg, r, axis=1, keepdims=False).reshape(bsz, nqb, NA_QB, H, Dh)
        s_loc = jnp.einsum("bjqhd,bwjkhd->bhjqwk", q_r, k_blk).astype(F32) * scale
        dr_idx = rs + jnp.arange(wr) - r + NA_WIN_R - 1
        bias = rpb_cols[:, dr_idx].transpose(0, 2, 3, 1, 4).astype(F32)
        s_loc = jnp.where(col_mask[:, :, None, :], s_loc + bias, NEG_INF)
        s_ctx = jnp.einsum("bjqhd,bchd->bhjqc", q_r, k_ctx).astype(F32) * scale
        s = jnp.concatenate([s_loc.reshape(bsz, H, nqb, NA_QB, n_loc), s_ctx], axis=-1)
        pr = jax.nn.softmax(s, axis=-1)
        p_loc = pr[..., :n_loc].reshape(bsz, H, nqb, NA_QB, wr, NA_KB)
        o = (jnp.einsum("bhjqwk,bwjkhd->bjqhd", p_loc, v_blk)
             + jnp.einsum("bhjqc,bchd->bjqhd", pr[..., n_loc:], v_ctx))
        return o.reshape(bsz, GRID_W, H, Dh)

    o = lax.map(row_block, jnp.arange(rows))
    return o.transpose(1, 0, 2, 3, 4).reshape(bsz, L, H * Dh)


def _complex_affine_combine(e1, e2):
    a1r, a1i, b1r, b1i = e1
    a2r, a2i, b2r, b2i = e2
    return (a2r * a1r - a2i * a1i, a2r * a1i + a2i * a1r,
            a2r * b1r - a2i * b1i + b2r, a2r * b1i + a2i * b1r + b2i)


def _s5_branch(u, lam_re, lam_im, log_step, b_re, b_im, c_re, c_im, d_skip, glu_w, glu_b, h0):
    bsz, L, _ = u.shape
    ug = u.reshape(bsz, L, S5_GROUPS, S5_CH)
    y = d_skip * u
    finals = []
    for d in range(2):
        step = jnp.exp(log_step[d])[:, None]
        lr, li = lam_re[d], lam_im[d]
        mag = jnp.exp(lr * step)
        ab_re, ab_im = mag * jnp.cos(li * step), mag * jnp.sin(li * step)
        den = lr * lr + li * li
        f_re = ((ab_re - 1) * lr + ab_im * li) / den
        f_im = (ab_im * lr - (ab_re - 1) * li) / den
        bb_re = f_re[..., None] * b_re - f_im[..., None] * b_im
        bb_im = f_re[..., None] * b_im + f_im[..., None] * b_re
        ud = ug if d == 0 else jnp.flip(ug, axis=1)
        bu_re = jnp.einsum("blgc,gpc->blgp", ud, bb_re)
        bu_im = jnp.einsum("blgc,gpc->blgp", ud, bb_im)
        if h0 is not None:
            h_re0, h_im0 = h0[:, d, 0], h0[:, d, 1]
            bu_re = bu_re.at[:, 0].add(ab_re * h_re0 - ab_im * h_im0)
            bu_im = bu_im.at[:, 0].add(ab_re * h_im0 + ab_im * h_re0)
        a_re = jnp.broadcast_to(ab_re, (1, L) + ab_re.shape)
        a_im = jnp.broadcast_to(ab_im, (1, L) + ab_im.shape)
        _, _, h_re, h_im = lax.associative_scan(_complex_affine_combine, (a_re, a_im, bu_re, bu_im), axis=1)
        yd = jnp.einsum("blgp,gcp->blgc", h_re, c_re[d]) - jnp.einsum("blgp,gcp->blgc", h_im, c_im[d])
        if d == 1:
            yd = jnp.flip(yd, axis=1)
        y = y + yd.reshape(bsz, L, D_S5)
        finals.append(jnp.stack([h_re[:, -1], h_im[:, -1]], axis=1))
    y = jax.nn.gelu(y)
    y = y * jax.nn.sigmoid(y @ glu_w + glu_b)
    return y, jnp.stack(finals, axis=1)


def _layer(x, mod, p, ctx_l):
    bf = lambda w: w.astype(BF16)
    cuts = np.cumsum((0, 3 * D_MODEL, D_SSD, SSD_CONV_CH, 2 * SSD_HEADS, D_NA, D_NA, D_NA, D_S5))
    w_in = p["w_in"]
    h = _norm_mod(x, p["norm1_g"], mod, shift_chunk=0, scale_chunk=1)

    def proj(idx, tn):
        w = bf(w_in[:, cuts[idx]:cuts[idx + 1]])
        return _matmul(h, w, tm=1024, tn=tn, out_dtype=F32, name=f"in_proj{idx}")

    gates = proj(0, 1024)
    z = proj(1, 1024)
    xbc = proj(2, 1024)
    dt_raw = proj(3, 128)
    q = proj(4, 1024)
    k = proj(5, 1024)
    v = proj(6, 1024)
    u = proj(7, 1024)

    k_ctx, v_ctx, h0_ssd, h0_s5 = ctx_l

    def split(a, width):
        return a[:M_CTX].reshape(BATCH, SEQ, width), a[M_CTX:].reshape(DEC_BATCH, DEC_SEQ, width)

    def join(a_ctx, a_lat):
        return jnp.concatenate([a_ctx.reshape(M_CTX, -1), a_lat.reshape(M_LAT, -1)], axis=0)

    z_c, z_l = split(z, D_SSD)
    xbc_c, xbc_l = split(xbc, SSD_CONV_CH)
    dt_c, dt_l = split(dt_raw, 2 * SSD_HEADS)
    ssd_args = (p["ssd_conv_w"], p["ssd_conv_b"], p["ssd_dt_bias"], p["ssd_a_log"], p["ssd_d"], p["ssd_norm_g"])
    o_ssd_c, st_ssd = _ssd_branch(z_c, xbc_c, dt_c, *ssd_args, None)
    o_ssd_l, _ = _ssd_branch(z_l, xbc_l, dt_l, *ssd_args, h0_ssd)
    o_ssd = join(o_ssd_c, o_ssd_l).astype(BF16)

    hd = lambda a, b, L: a.reshape(b, L, NA_HEADS, NA_HEAD_DIM)
    q_c, q_l = split(q, D_NA)
    k_c, k_l = split(k, D_NA)
    v_c, v_l = split(v, D_NA)
    qn_c = _rmsnorm(hd(q_c, BATCH, SEQ), p["na_q_g"])
    kn_c = _rmsnorm(hd(k_c, BATCH, SEQ), p["na_k_g"])
    qn_l = _rmsnorm(hd(q_l, DEC_BATCH, DEC_SEQ), p["na_q_g"])
    kn_l = _rmsnorm(hd(k_l, DEC_BATCH, DEC_SEQ), p["na_k_g"])
    o_na_c = _context_attn(qn_c, kn_c, hd(v_c, BATCH, SEQ))
    o_na_l = _neighbourhood_attn(qn_l, kn_l, hd(v_l, DEC_BATCH, DEC_SEQ), k_ctx, v_ctx, p["na_rpb"])
    o_na = join(o_na_c, o_na_l).astype(BF16)

    u_c, u_l = split(u, D_S5)
    s5_args = (p["s5_lam_re"], p["s5_lam_im"], p["s5_log_step"], p["s5_b_re"], p["s5_b_im"], p["s5_c_re"],
               p["s5_c_im"], p["s5_d"], p["s5_glu_w"], p["s5_glu_b"])
    o_s5_c, st_s5 = _s5_branch(u_c, *s5_args, None)
    o_s5_l, _ = _s5_branch(u_l, *s5_args, h0_s5)
    o_s5 = join(o_s5_c, o_s5_l).astype(BF16)

    merged = _merge(o_ssd, o_na, o_s5, bf(p["w_po_ssd"]), bf(p["w_po_na"]), bf(p["w_po_s5"]), gates)
    x = _matmul(merged, bf(p["w_o"]), tm=1024, tn=1024, out_dtype=F32, res=x, mod=mod, gate_chunk=2, name="w_o")
    h2 = _norm_mod(x, p["norm2_g"], mod, shift_chunk=3, scale_chunk=4)
    a = _matmul(h2, bf(p["mlp_w1"]), tm=1024, tn=1024, out_dtype=BF16, act="relu2", name="mlp1")
    x = _matmul(a, bf(p["mlp_w2"]), tm=1024, tn=1024, tk=2048, out_dtype=F32, res=x, mod=mod, gate_chunk=5,
                name="mlp2")
    return x, (kn_c, hd(v_c, BATCH, SEQ), st_ssd, st_s5)


def kernel(x_prompt, x_sample, c, cache_k, cache_v, state_ssd, state_s5, c_ctx, ada_w, ada_b, norm1_g, norm2_g, w_in, ssd_conv_w, ssd_conv_b, ssd_dt_bias, ssd_a_log, ssd_d, ssd_norm_g, na_q_g, na_k_g, na_rpb, s5_lam_re, s5_lam_im, s5_log_step, s5_b_re, s5_b_im, s5_c_re, s5_c_im, s5_d, s5_glu_w, s5_glu_b, w_po_ssd, w_po_na, w_po_s5, w_o, mlp_w1, mlp_w2):
    params = dict(norm1_g=norm1_g, norm2_g=norm2_g, w_in=w_in, ssd_conv_w=ssd_conv_w, ssd_conv_b=ssd_conv_b,
                  ssd_dt_bias=ssd_dt_bias, ssd_a_log=ssd_a_log, ssd_d=ssd_d, ssd_norm_g=ssd_norm_g,
                  na_q_g=na_q_g, na_k_g=na_k_g, na_rpb=na_rpb, s5_lam_re=s5_lam_re, s5_lam_im=s5_lam_im,
                  s5_log_step=s5_log_step, s5_b_re=s5_b_re, s5_b_im=s5_b_im, s5_c_re=s5_c_re, s5_c_im=s5_c_im,
                  s5_d=s5_d, s5_glu_w=s5_glu_w, s5_glu_b=s5_glu_b, w_po_ssd=w_po_ssd, w_po_na=w_po_na,
                  w_po_s5=w_po_s5, w_o=w_o, mlp_w1=mlp_w1, mlp_w2=mlp_w2)
    cvec = jnp.concatenate([c_ctx[None, :], c, jnp.zeros((MOD_ROWS - 1 - DEC_BATCH, D_MODEL), F32)], axis=0)
    mods = _ada_mod(cvec, ada_w, ada_b)
    x = jnp.concatenate([x_prompt.reshape(M_CTX, D_MODEL), x_sample.reshape(M_LAT, D_MODEL)], axis=0)
    ks, vs, ssds, s5s = [], [], [], []
    for l in range(DEPTH):
        p = {name: val[l] for name, val in params.items()}
        mod = mods[l].reshape(MOD_ROWS, 1, N_MOD * D_MODEL)
        ctx_l = (cache_k[:, l], cache_v[:, l], state_ssd[:, l], state_s5[:, l])
        x, (k_l, v_l, ssd_l, s5_l) = _layer(x, mod, p, ctx_l)
        ks.append(k_l)
        vs.append(v_l)
        ssds.append(ssd_l)
        s5s.append(s5_l)
    y_prompt = x[:M_CTX].reshape(BATCH, SEQ, D_MODEL)
    y_sample = x[M_CTX:].reshape(DEC_BATCH, DEC_SEQ, D_MODEL)
    return (y_prompt, y_sample, jnp.stack(ks, axis=1), jnp.stack(vs, axis=1),
            jnp.stack(ssds, axis=1), jnp.stack(s5s, axis=1))
```

```python
import functools

import jax
import jax.numpy as jnp
import numpy as np
from jax import lax
from jax.experimental import pallas as pl
from jax.experimental.pallas import tpu as pltpu

F32 = jnp.float32
BF16 = jnp.bfloat16

D_MODEL = 4096
BATCH = 16
SEQ = 256
DEPTH = 2
DEC_BATCH = 4
DEC_SEQ = 2048
PAST_LEN = 256
GRID_W = 64
GRID_ROWS = DEC_SEQ // GRID_W
D_SSD = D_MODEL
SSD_HEAD_DIM = 64
SSD_HEADS = D_SSD // SSD_HEAD_DIM
SSD_GROUPS = 8
SSD_STATE = 128
SSD_GN = SSD_GROUPS * SSD_STATE
SSD_CONV_W = 5
SSD_CONV_CH = D_SSD + 2 * SSD_GN
SSD_CHUNK = 128
D_NA = D_MODEL // 2
NA_HEAD_DIM = 128
NA_HEADS = D_NA // NA_HEAD_DIM
NA_WIN_R = 8
NA_WIN_C = 16
D_S5 = D_MODEL // 2
S5_CH = 16
S5_GROUPS = D_S5 // S5_CH
S5_STATE = 64
D_FF = 4 * D_MODEL
N_MOD = 6
EPS = 1e-6
NEG_INF = -1e30

M_CTX = BATCH * SEQ
M_LAT = DEC_BATCH * DEC_SEQ
M_ALL = M_CTX + M_LAT
N_SEQ_ALL = BATCH + DEC_BATCH
MOD_ROWS = 8

LANES = 128
SUBLANES = 8
VMEM_LIMIT = 56 * 1024 * 1024


def _cparams(*sem):
    return pltpu.CompilerParams(dimension_semantics=sem, vmem_limit_bytes=VMEM_LIMIT)


def _mod_row(i, tm):
    n_ctx = M_CTX // tm
    per_seq = DEC_SEQ // tm
    return jnp.where(i < n_ctx, 0, 1 + (i - n_ctx) // per_seq)


def _seq_pos(r, rows):
    n_ctx = M_CTX // rows
    per_ctx = SEQ // rows
    per_lat = DEC_SEQ // rows
    is_lat = r >= n_ctx
    rl = jnp.where(is_lat, r - n_ctx, 0)
    pos = jnp.where(is_lat, rl % per_lat, r % per_ctx)
    first = pos == 0
    last = jnp.where(is_lat, pos == per_lat - 1, pos == per_ctx - 1)
    seq = jnp.where(is_lat, BATCH + rl // per_lat, r // per_ctx)
    return is_lat, first, last, seq


def _silu(x):
    return x * jax.nn.sigmoid(x)


def _ada_kernel(c_ref, w_ref, b_ref, o_ref):
    s = _silu(c_ref[...]).astype(BF16)
    acc = jnp.dot(s, w_ref[0].astype(BF16), preferred_element_type=F32)
    o_ref[0] = acc + b_ref[0]


def _ada_mod(cvec, ada_w, ada_b):
    n = N_MOD * D_MODEL
    tn = 512
    return pl.pallas_call(
        _ada_kernel,
        grid=(DEPTH, n // tn),
        in_specs=[pl.BlockSpec((MOD_ROWS, D_MODEL), lambda l, j: (0, 0)),
                  pl.BlockSpec((1, D_MODEL, tn), lambda l, j: (l, 0, j)),
                  pl.BlockSpec((1, 1, tn), lambda l, j: (l, 0, j))],
        out_specs=pl.BlockSpec((1, MOD_ROWS, tn), lambda l, j: (l, 0, j)),
        out_shape=jax.ShapeDtypeStruct((DEPTH, MOD_ROWS, n), F32),
        compiler_params=_cparams("parallel", "parallel"),
        name="ada_mod",
    )(cvec, ada_w, ada_b.reshape(DEPTH, 1, n))


def _norm_mod_kernel(x_ref, g_ref, sc_ref, sh_ref, o_ref):
    x = x_ref[...]
    ms = jnp.mean(x * x, axis=-1, keepdims=True)
    y = x * lax.rsqrt(ms + EPS) * g_ref[...]
    o_ref[...] = (y * (1.0 + sc_ref[0]) + sh_ref[0]).astype(o_ref.dtype)


def _norm_mod(x, g, mod, shift_chunk, scale_chunk):
    tm = 256
    return pl.pallas_call(
        _norm_mod_kernel,
        grid=(M_ALL // tm,),
        in_specs=[pl.BlockSpec((tm, D_MODEL), lambda i: (i, 0)),
                  pl.BlockSpec((1, D_MODEL), lambda i: (0, 0)),
                  pl.BlockSpec((1, 1, D_MODEL), lambda i: (_mod_row(i, tm), 0, scale_chunk)),
                  pl.BlockSpec((1, 1, D_MODEL), lambda i: (_mod_row(i, tm), 0, shift_chunk))],
        out_specs=pl.BlockSpec((tm, D_MODEL), lambda i: (i, 0)),
        out_shape=jax.ShapeDtypeStruct((M_ALL, D_MODEL), BF16),
        compiler_params=_cparams("parallel"),
        name="norm_mod",
    )(x, g.reshape(1, D_MODEL), mod, mod)


def _mm_kernel(*refs, nk, act, has_res):
    if has_res:
        x_ref, w_ref, res_ref, gt_ref, o_ref = refs[:5]
        scratch = refs[5:]
    else:
        x_ref, w_ref, o_ref = refs[:3]
        scratch = refs[3:]

    def finish(v):
        if act == "relu2":
            r = jnp.maximum(v, 0.0)
            v = r * r
        if has_res:
            v = res_ref[...] + gt_ref[0] * v
        o_ref[...] = v.astype(o_ref.dtype)

    part = jnp.dot(x_ref[...], w_ref[...], preferred_element_type=F32)
    if nk == 1:
        finish(part)
    else:
        acc_ref, = scratch
        k = pl.program_id(2)

        @pl.when(k == 0)
        def _():
            acc_ref[...] = part

        @pl.when(k > 0)
        def _():
            acc_ref[...] += part

        @pl.when(k == nk - 1)
        def _():
            finish(acc_ref[...])


def _matmul(x, w, *, tm, tn, tk=None, out_dtype, act=None, res=None, mod=None, gate_chunk=None, name):
    m, kdim = x.shape
    n = w.shape[1]
    tk = kdim if tk is None else tk
    nk = kdim // tk
    has_res = res is not None
    in_specs = [pl.BlockSpec((tm, tk), lambda i, j, k: (i, k)),
                pl.BlockSpec((tk, tn), lambda i, j, k: (k, j))]
    args = [x, w]
    if has_res:
        nj = n // tn
        in_specs += [pl.BlockSpec((tm, tn), lambda i, j, k: (i, j)),
                     pl.BlockSpec((1, 1, tn), lambda i, j, k: (_mod_row(i, tm), 0, gate_chunk * nj + j))]
        args += [res, mod]
    return pl.pallas_call(
        functools.partial(_mm_kernel, nk=nk, act=act, has_res=has_res),
        grid=(m // tm, n // tn, nk),
        in_specs=in_specs,
        out_specs=pl.BlockSpec((tm, tn), lambda i, j, k: (i, j)),
        out_shape=jax.ShapeDtypeStruct((m, n), out_dtype),
        scratch_shapes=[pltpu.VMEM((tm, tn), F32)] if nk > 1 else [],
        compiler_params=_cparams("parallel", "parallel", "arbitrary"),
        name=name,
    )(*args)


def _merge_kernel(o1_ref, o2_ref, o3_ref, w1_ref, w2_ref, w3_ref, g1_ref, g2_ref, g3_ref, out_ref):
    acc = jax.nn.sigmoid(g1_ref[...]) * jnp.dot(o1_ref[...], w1_ref[...], preferred_element_type=F32)
    acc += jax.nn.sigmoid(g2_ref[...]) * jnp.dot(o2_ref[...], w2_ref[...], preferred_element_type=F32)
    acc += jax.nn.sigmoid(g3_ref[...]) * jnp.dot(o3_ref[...], w3_ref[...], preferred_element_type=F32)
    out_ref[...] = acc.astype(out_ref.dtype)


def _merge(o_ssd, o_na, o_s5, w1, w2, w3, gates):
    tm, tn = 512, 512
    nj = D_MODEL // tn
    return pl.pallas_call(
        _merge_kernel,
        grid=(M_ALL // tm, nj),
        in_specs=[pl.BlockSpec((tm, D_SSD), lambda i, j: (i, 0)),
                  pl.BlockSpec((tm, D_NA), lambda i, j: (i, 0)),
                  pl.BlockSpec((tm, D_S5), lambda i, j: (i, 0)),
                  pl.BlockSpec((D_SSD, tn), lambda i, j: (0, j)),
                  pl.BlockSpec((D_NA, tn), lambda i, j: (0, j)),
                  pl.BlockSpec((D_S5, tn), lambda i, j: (0, j)),
                  pl.BlockSpec((tm, tn), lambda i, j: (i, j)),
                  pl.BlockSpec((tm, tn), lambda i, j: (i, nj + j)),
                  pl.BlockSpec((tm, tn), lambda i, j: (i, 2 * nj + j))],
        out_specs=pl.BlockSpec((tm, tn), lambda i, j: (i, j)),
        out_shape=jax.ShapeDtypeStruct((M_ALL, D_MODEL), BF16),
        compiler_params=_cparams("parallel", "parallel"),
        name="merge",
    )(o_ssd, o_na, o_s5, w1, w2, w3, gates, gates, gates)


CONV_TM = SEQ
CONV_HALO = SUBLANES


def _conv_kernel(prev_ref, x_ref, next_ref, w_ref, b_ref, o_ref):
    _, first, last, _ = _seq_pos(pl.program_id(0), CONV_TM)
    prev = jnp.where(first, 0.0, prev_ref[...])
    nxt = jnp.where(last, 0.0, next_ref[...])
    ext = jnp.concatenate([prev, x_ref[...], nxt], axis=0)
    acc = b_ref[...]
    for k in range(SSD_CONV_W):
        off = CONV_HALO - SSD_CONV_W // 2 + k
        acc = acc + w_ref[k:k + 1, :] * ext[off:off + CONV_TM]
    o_ref[...] = _silu(acc)


def _conv_silu(xbc, w, b, col0, width):
    ct = 512
    joff = col0 // ct
    halo_per = CONV_TM // CONV_HALO
    n_halo = M_ALL // CONV_HALO
    return pl.pallas_call(
        _conv_kernel,
        grid=(M_ALL // CONV_TM, width // ct),
        in_specs=[pl.BlockSpec((CONV_HALO, ct), lambda i, j: (jnp.maximum(i * halo_per - 1, 0), joff + j)),
                  pl.BlockSpec((CONV_TM, ct), lambda i, j: (i, joff + j)),
                  pl.BlockSpec((CONV_HALO, ct), lambda i, j: (jnp.minimum((i + 1) * halo_per, n_halo - 1), joff + j)),
                  pl.BlockSpec((SSD_CONV_W, ct), lambda i, j: (0, joff + j)),
                  pl.BlockSpec((1, ct), lambda i, j: (0, joff + j))],
        out_specs=pl.BlockSpec((CONV_TM, ct), lambda i, j: (i, j)),
        out_shape=jax.ShapeDtypeStruct((M_ALL, width), F32),
        compiler_params=_cparams("parallel", "parallel"),
        name="ssd_conv",
    )(xbc, xbc, xbc, w, b.reshape(1, -1))


SSD_NCH = M_ALL // SSD_CHUNK
SSD_PAIRS = SSD_HEADS // 2
HEADS_PER_GROUP = SSD_HEADS // SSD_GROUPS


def _cumsum_rows(a):
    t = a.shape[0]
    row = lax.broadcasted_iota(jnp.int32, (t, t), 0)
    col = lax.broadcasted_iota(jnp.int32, (t, t), 1)
    tri = (col <= row).astype(BF16)
    hi = a.astype(BF16)
    r1 = a - hi.astype(F32)
    mid = r1.astype(BF16)
    lo = (r1 - mid.astype(F32)).astype(BF16)
    return (jnp.dot(tri, hi, preferred_element_type=F32) + jnp.dot(tri, mid, preferred_element_type=F32)
            + jnp.dot(tri, lo, preferred_element_type=F32))


def _ssd_kernel(x_ref, b_ref, c_ref, dt_ref, dtb_ref, alog_ref, h0_ref, y_ref, fin_ref, st_ref, aux_ref, *, d):
    T = SSD_CHUNK
    i = pl.program_id(0)
    r = i if d == 0 else SSD_NCH - 1 - i
    is_lat, seq_first, seq_last, _ = _seq_pos(r, T)
    first = seq_first if d == 0 else seq_last
    last = seq_last if d == 0 else seq_first

    @pl.when(first & is_lat)
    def _():
        st_ref[...] = h0_ref[0]

    @pl.when(first & jnp.logical_not(is_lat))
    def _():
        st_ref[...] = jnp.zeros_like(st_ref)

    dt = jax.nn.softplus(dt_ref[...] + dtb_ref[...])
    a = dt * (-jnp.exp(alog_ref[...]))
    cs = _cumsum_rows(a)
    tot = cs[T - 1:T, :]
    if d == 0:
        pot = cs
        wexp = jnp.exp(tot - cs)
    else:
        ecs = cs - a
        pot = -ecs
        wexp = jnp.exp(ecs)
    aux_ref[0] = pot.T
    aux_ref[1] = dt.T
    aux_ref[2] = (dt * wexp).T
    aux_ref[3] = jnp.broadcast_to(tot, (T, T)).T

    row = lax.broadcasted_iota(jnp.int32, (T, T), 0)
    col = lax.broadcasted_iota(jnp.int32, (T, T), 1)
    mask = (row >= col) if d == 0 else (col >= row)
    lane_lo = lax.broadcasted_iota(jnp.int32, (T, LANES), 1) < SSD_HEAD_DIM

    for g in range(SSD_GROUPS):
        gs = slice(g * SSD_STATE, (g + 1) * SSD_STATE)
        bg = b_ref[:, gs]
        cg = c_ref[:, gs]
        cb = lax.dot_general(cg.astype(BF16), bg.astype(BF16), (((1,), (1,)), ((), ())),
                             preferred_element_type=F32)
        bgt = bg.T
        for pp in range(HEADS_PER_GROUP // 2):
            pair = g * (HEADS_PER_GROUP // 2) + pp
            ps = slice(pair * LANES, (pair + 1) * LANES)
            xp = x_ref[:, ps].astype(BF16)
            st = st_ref[pair]
            st16 = st.astype(BF16)
            ys, incs, decs = [], [], []
            for e in range(2):
                h = d * SSD_HEADS + pair * 2 + e
                colb = jnp.broadcast_to(pot[:, h:h + 1], (T, T))
                lm = jnp.exp(jnp.where(mask, colb - aux_ref[0, h:h + 1, :], NEG_INF))
                mh = (cb * lm * aux_ref[1, h:h + 1, :]).astype(BF16)
                totrow = aux_ref[3, h:h + 1, :]
                oh = jnp.exp(colb) if d == 0 else jnp.exp(colb + totrow)
                co = (cg * oh).astype(BF16)
                ys.append(jnp.dot(mh, xp, preferred_element_type=F32)
                          + jnp.dot(co, st16, preferred_element_type=F32))
                btw = (bgt * aux_ref[2, h:h + 1, :]).astype(BF16)
                incs.append(jnp.dot(btw, xp, preferred_element_type=F32))
                decs.append(jnp.exp(totrow))
            y_ref[:, ps] = jnp.where(lane_lo, ys[0], ys[1])
            dec = jnp.where(lane_lo[:1], decs[0], decs[1])
            st_ref[pair] = dec * st + jnp.where(lane_lo, incs[0], incs[1])

    @pl.when(last)
    def _():
        fin_ref[0] = st_ref[...]


def _ssd_scan(xs, bm, cm, dt_raw, dt_bias, a_log, h0, d):
    T = SSD_CHUNK

    def rblk(i):
        return i if d == 0 else SSD_NCH - 1 - i

    def seq_of(i):
        return _seq_pos(rblk(i), T)[3]

    return pl.pallas_call(
        functools.partial(_ssd_kernel, d=d),
        grid=(SSD_NCH,),
        in_specs=[pl.BlockSpec((T, D_SSD), lambda i: (rblk(i), 0)),
                  pl.BlockSpec((T, SSD_GN), lambda i: (rblk(i), 0)),
                  pl.BlockSpec((T, SSD_GN), lambda i: (rblk(i), 0)),
                  pl.BlockSpec((T, 2 * SSD_HEADS), lambda i: (rblk(i), 0)),
                  pl.BlockSpec((1, 2 * SSD_HEADS), lambda i: (0, 0)),
                  pl.BlockSpec((1, 2 * SSD_HEADS), lambda i: (0, 0)),
                  pl.BlockSpec((1, SSD_PAIRS, SSD_STATE, LANES),
                               lambda i: (jnp.maximum(seq_of(i) - BATCH, 0), 0, 0, 0))],
        out_specs=[pl.BlockSpec((T, D_SSD), lambda i: (rblk(i), 0)),
                   pl.BlockSpec((1, SSD_PAIRS, SSD_STATE, LANES), lambda i: (seq_of(i), 0, 0, 0))],
        out_shape=[jax.ShapeDtypeStruct((M_ALL, D_SSD), F32),
                   jax.ShapeDtypeStruct((N_SEQ_ALL, SSD_PAIRS, SSD_STATE, LANES), F32)],
        scratch_shapes=[pltpu.VMEM((SSD_PAIRS, SSD_STATE, LANES), F32),
                        pltpu.VMEM((4, T, T), F32)],
        compiler_params=_cparams("arbitrary"),
        name=f"ssd_scan{d}",
    )(xs, bm, cm, dt_raw, dt_bias.reshape(1, -1), a_log.reshape(1, -1), h0)


def _ssd_out_kernel(yf_ref, yb_ref, xs_ref, z_ref, d_ref, g_ref, o_ref):
    y = (yf_ref[...] + yb_ref[...] + d_ref[...] * xs_ref[...]) * _silu(z_ref[...])
    gw = D_SSD // SSD_GROUPS
    for g in range(SSD_GROUPS):
        gs = slice(g * gw, (g + 1) * gw)
        yg = y[:, gs]
        ms = jnp.mean(yg * yg, axis=-1, keepdims=True)
        o_ref[:, gs] = (yg * lax.rsqrt(ms + EPS) * g_ref[:, gs]).astype(o_ref.dtype)


def _ssd_out(yf, yb, xs, z, d_skip, norm_g):
    tm = 256
    row = pl.BlockSpec((tm, D_SSD), lambda i: (i, 0))
    vec = pl.BlockSpec((1, D_SSD), lambda i: (0, 0))
    return pl.pallas_call(
        _ssd_out_kernel,
        grid=(M_ALL // tm,),
        in_specs=[row, row, row, row, vec, vec],
        out_specs=row,
        out_shape=jax.ShapeDtypeStruct((M_ALL, D_SSD), BF16),
        compiler_params=_cparams("parallel"),
        name="ssd_out",
    )(yf, yb, xs, z, jnp.repeat(d_skip, SSD_HEAD_DIM).reshape(1, D_SSD), norm_g.reshape(1, D_SSD))


def _ssd_state_to_pairs(s):
    b = s.shape[0]
    s = s.reshape(b, SSD_PAIRS, 2, SSD_HEAD_DIM, SSD_STATE)
    return s.transpose(0, 1, 4, 2, 3).reshape(b, SSD_PAIRS, SSD_STATE, LANES)


def _ssd_state_from_pairs(s):
    b = s.shape[0]
    s = s.reshape(b, SSD_PAIRS, SSD_STATE, 2, SSD_HEAD_DIM)
    return s.transpose(0, 1, 3, 4, 2).reshape(b, SSD_HEADS, SSD_HEAD_DIM, SSD_STATE)


def _head_rms(x, g):
    return x * lax.rsqrt(jnp.mean(x * x, axis=-1, keepdims=True) + EPS) * g


ATT_SCALE = NA_HEAD_DIM ** -0.5
CTX_HEADS_PER_STEP = 4


def _ctx_attn_kernel(q_ref, k_ref, v_ref, qg_ref, kg_ref, o_ref, kn_ref):
    for hh in range(CTX_HEADS_PER_STEP):
        hs = slice(hh * NA_HEAD_DIM, (hh + 1) * NA_HEAD_DIM)
        qn = _head_rms(q_ref[:, hs], qg_ref[...])
        kn = _head_rms(k_ref[:, hs], kg_ref[...])
        kn_ref[:, hs] = kn
        s = lax.dot_general(qn.astype(BF16), kn.astype(BF16), (((1,), (1,)), ((), ())),
                            preferred_element_type=F32) * ATT_SCALE
        p = jnp.exp(s - jnp.max(s, axis=-1, keepdims=True))
        den = jnp.sum(p, axis=-1, keepdims=True)
        o = jnp.dot(p.astype(BF16), v_ref[:, hs].astype(BF16), preferred_element_type=F32)
        o_ref[:, hs] = (o / den).astype(o_ref.dtype)


def _ctx_attn(q, k, v, q_g, k_g):
    w = CTX_HEADS_PER_STEP * NA_HEAD_DIM
    blk = pl.BlockSpec((SEQ, w), lambda b, j: (b, j))
    vec = pl.BlockSpec((1, NA_HEAD_DIM), lambda b, j: (0, 0))
    return pl.pallas_call(
        _ctx_attn_kernel,
        grid=(BATCH, D_NA // w),
        in_specs=[blk, blk, blk, vec, vec],
        out_specs=[blk, blk],
        out_shape=[jax.ShapeDtypeStruct((M_CTX, D_NA), BF16), jax.ShapeDtypeStruct((M_CTX, D_NA), F32)],
        compiler_params=_cparams("parallel", "parallel"),
        name="ctx_attn",
    )(q, k, v, q_g.reshape(1, -1), k_g.reshape(1, -1))


NA_BAND_ROWS = 8
NA_BANDS = GRID_ROWS // NA_BAND_ROWS
NA_KEY_ROWS = 2 * NA_BAND_ROWS
NA_BAND_Q = NA_BAND_ROWS * GRID_W
NA_BAND_K = NA_KEY_ROWS * GRID_W


def _na_band_key_row0(band):
    return np.clip(band * NA_BAND_ROWS - NA_WIN_R // 2, 0, GRID_ROWS - NA_KEY_ROWS)


def _na_bias_table(rpb):
    qc = np.arange(GRID_W)[:, None]
    kc = np.arange(GRID_W)[None, :]
    ws = np.clip(qc - NA_WIN_C // 2, 0, GRID_W - NA_WIN_C)
    col_ok = (kc >= ws) & (kc < ws + NA_WIN_C)
    dc = np.clip(kc - qc + NA_WIN_C - 1, 0, 2 * NA_WIN_C - 2)
    band = np.arange(NA_BANDS)[:, None, None]
    qr = band * NA_BAND_ROWS + np.arange(NA_BAND_ROWS)[None, :, None]
    kr = _na_band_key_row0(band) + np.arange(NA_KEY_ROWS)[None, None, :]
    rs = np.clip(qr - NA_WIN_R // 2, 0, GRID_ROWS - NA_WIN_R)
    row_ok = (kr >= rs) & (kr < rs + NA_WIN_R)
    dr = np.clip(kr - qr + NA_WIN_R - 1, 0, 2 * NA_WIN_R - 2)
    cols = rpb[:, :, dc]
    t = cols[:, dr]
    ok = row_ok[None, :, :, :, None, None] & col_ok[None, None, None, None, :, :]
    t = jnp.where(ok, t, NEG_INF)
    return t.transpose(0, 1, 2, 4, 3, 5).reshape(NA_HEADS, NA_BANDS, NA_BAND_Q, NA_BAND_K)


def _lat_attn_kernel(q_ref, k_ref, v_ref, kc_ref, vc_ref, bias_ref, qg_ref, kg_ref, o_ref):
    band = pl.program_id(2)
    row0 = jnp.clip(band * NA_BAND_ROWS - NA_WIN_R // 2, 0, GRID_ROWS - NA_KEY_ROWS)
    k0 = pl.multiple_of(row0 * GRID_W, NA_WIN_R // 2 * GRID_W)
    qn = _head_rms(q_ref[...], qg_ref[...]).astype(BF16)
    kn = _head_rms(k_ref[pl.ds(k0, NA_BAND_K), :], kg_ref[...]).astype(BF16)
    vw = v_ref[pl.ds(k0, NA_BAND_K), :].astype(BF16)
    nt = (((1,), (1,)), ((), ()))
    s_loc = lax.dot_general(qn, kn, nt, preferred_element_type=F32) * ATT_SCALE + bias_ref[0, 0]
    s_ctx = lax.dot_general(qn, kc_ref[0].astype(BF16), nt, preferred_element_type=F32) * ATT_SCALE
    m = jnp.maximum(jnp.max(s_loc, axis=-1, keepdims=True), jnp.max(s_ctx, axis=-1, keepdims=True))
    p_loc = jnp.exp(s_loc - m)
    p_ctx = jnp.exp(s_ctx - m)
    den = jnp.sum(p_loc, axis=-1, keepdims=True) + jnp.sum(p_ctx, axis=-1, keepdims=True)
    o = (jnp.dot(p_loc.astype(BF16), vw, preferred_element_type=F32)
         + jnp.dot(p_ctx.astype(BF16), vc_ref[0].astype(BF16), preferred_element_type=F32))
    o_ref[...] = (o / den).astype(o_ref.dtype)


def _lat_attn(q, k, v, k_ctx, v_ctx, bias, q_g, k_g):
    q_blk0 = M_CTX // NA_BAND_Q
    seq_blk0 = M_CTX // DEC_SEQ
    hd = NA_HEAD_DIM
    vec = pl.BlockSpec((1, hd), lambda b, h, n: (0, 0))
    return pl.pallas_call(
        _lat_attn_kernel,
        grid=(DEC_BATCH, NA_HEADS, NA_BANDS),
        in_specs=[pl.BlockSpec((NA_BAND_Q, hd), lambda b, h, n: (q_blk0 + b * NA_BANDS + n, h)),
                  pl.BlockSpec((DEC_SEQ, hd), lambda b, h, n: (seq_blk0 + b, h)),
                  pl.BlockSpec((DEC_SEQ, hd), lambda b, h, n: (seq_blk0 + b, h)),
                  pl.BlockSpec((1, PAST_LEN, hd), lambda b, h, n: (b, 0, h)),
                  pl.BlockSpec((1, PAST_LEN, hd), lambda b, h, n: (b, 0, h)),
                  pl.BlockSpec((1, 1, NA_BAND_Q, NA_BAND_K), lambda b, h, n: (h, n, 0, 0)),
                  vec, vec],
        out_specs=pl.BlockSpec((NA_BAND_Q, hd), lambda b, h, n: (b * NA_BANDS + n, h)),
        out_shape=jax.ShapeDtypeStruct((M_LAT, D_NA), BF16),
        compiler_params=_cparams("parallel", "parallel", "arbitrary"),
        name="lat_attn",
    )(q, k, v, k_ctx, v_ctx, bias, q_g.reshape(1, -1), k_g.reshape(1, -1))


S5_T = 128
S5_SEQS = 4
S5_ROWS = 2 * S5_SEQS
S5_GB = LANES // S5_CH
S5_NBLK = D_S5 // LANES
S5_SW = S5_GB * S5_STATE
S5_CB = S5_SW // LANES


def _s5_kernel(*refs, has_h0):
    if has_h0:
        (uf_ref, ub_ref, bre_ref, bim_ref, cre_ref, cim_ref, are_ref, aim_ref, h0re_ref, h0im_ref,
         yf_ref, yb_ref, fre_ref, fim_ref, bure_ref, buim_ref, hre_ref, him_ref) = refs
    else:
        (uf_ref, ub_ref, bre_ref, bim_ref, cre_ref, cim_ref, are_ref, aim_ref,
         yf_ref, yb_ref, fre_ref, fim_ref, bure_ref, buim_ref, hre_ref, him_ref) = refs
    T = S5_T
    i = pl.program_id(2)

    @pl.when(i == 0)
    def _():
        if has_h0:
            for cb in range(S5_CB):
                hre_ref[cb] = h0re_ref[0, :, cb * LANES:(cb + 1) * LANES]
                him_ref[cb] = h0im_ref[0, :, cb * LANES:(cb + 1) * LANES]
        else:
            hre_ref[...] = jnp.zeros_like(hre_ref)
            him_ref[...] = jnp.zeros_like(him_ref)

    row = lax.broadcasted_iota(jnp.int32, (T, T), 0)
    col = lax.broadcasted_iota(jnp.int32, (T, T), 1)
    flip = (row + col == T - 1).astype(BF16)

    def put(ref, rows, val):
        for cb in range(S5_CB):
            ref[cb, rows, :] = val[:, cb * LANES:(cb + 1) * LANES]

    def get(ref, rows):
        return jnp.concatenate([ref[cb, rows, :] for cb in range(S5_CB)], axis=1)

    for s in range(S5_SEQS):
        uf = uf_ref[0, s].astype(BF16)
        ub = jnp.dot(flip, ub_ref[0, s].astype(BF16), preferred_element_type=F32).astype(BF16)
        rf = slice(s * T, (s + 1) * T)
        rb = slice((S5_SEQS + s) * T, (S5_SEQS + s + 1) * T)
        put(bure_ref, rf, jnp.dot(uf, bre_ref[0, 0], preferred_element_type=F32))
        put(buim_ref, rf, jnp.dot(uf, bim_ref[0, 0], preferred_element_type=F32))
        put(bure_ref, rb, jnp.dot(ub, bre_ref[1, 0], preferred_element_type=F32))
        put(buim_ref, rb, jnp.dot(ub, bim_ref[1, 0], preferred_element_type=F32))

    a_re = [are_ref[0, :, cb * LANES:(cb + 1) * LANES] for cb in range(S5_CB)]
    a_im = [aim_ref[0, :, cb * LANES:(cb + 1) * LANES] for cb in range(S5_CB)]

    def step(t, carry):
        h_re, h_im = carry
        idx = pl.ds(t, S5_ROWS, stride=T)
        n_re, n_im = [], []
        for cb in range(S5_CB):
            r = a_re[cb] * h_re[cb] - a_im[cb] * h_im[cb] + bure_ref[cb, idx, :]
            m = a_re[cb] * h_im[cb] + a_im[cb] * h_re[cb] + buim_ref[cb, idx, :]
            bure_ref[cb, idx, :] = r
            buim_ref[cb, idx, :] = m
            n_re.append(r)
            n_im.append(m)
        return tuple(n_re), tuple(n_im)

    init = (tuple(hre_ref[cb] for cb in range(S5_CB)), tuple(him_ref[cb] for cb in range(S5_CB)))
    h_re, h_im = lax.fori_loop(0, T, step, init)
    for cb in range(S5_CB):
        hre_ref[cb] = h_re[cb]
        him_ref[cb] = h_im[cb]
        fre_ref[0, 0, :, cb * LANES:(cb + 1) * LANES] = h_re[cb]
        fim_ref[0, 0, :, cb * LANES:(cb + 1) * LANES] = h_im[cb]

    for s in range(S5_SEQS):
        rf = slice(s * T, (s + 1) * T)
        rb = slice((S5_SEQS + s) * T, (S5_SEQS + s + 1) * T)
        yf_ref[0, s] = (jnp.dot(get(bure_ref, rf).astype(BF16), cre_ref[0, 0], preferred_element_type=F32)
                        - jnp.dot(get(buim_ref, rf).astype(BF16), cim_ref[0, 0], preferred_element_type=F32))
        yb = (jnp.dot(get(bure_ref, rb).astype(BF16), cre_ref[1, 0], preferred_element_type=F32)
              - jnp.dot(get(buim_ref, rb).astype(BF16), cim_ref[1, 0], preferred_element_type=F32))
        hi = yb.astype(BF16)
        r1 = yb - hi.astype(F32)
        mid = r1.astype(BF16)
        lo = (r1 - mid.astype(F32)).astype(BF16)
        yb_ref[0, s] = (jnp.dot(flip, hi, preferred_element_type=F32) + jnp.dot(flip, mid, preferred_element_type=F32)
                        + jnp.dot(flip, lo, preferred_element_type=F32))


def _s5_scan(u4, mats, h0):
    nset, _, L, _ = u4.shape
    nt = L // S5_T
    bre, bim, cre, cim, are, aim = mats
    has_h0 = h0 is not None
    ublk = (1, S5_SEQS, S5_T, LANES)
    in_specs = [pl.BlockSpec(ublk, lambda s, j, i: (s, 0, i, j)),
                pl.BlockSpec(ublk, lambda s, j, i: (s, 0, nt - 1 - i, j)),
                pl.BlockSpec((2, 1, LANES, S5_SW), lambda s, j, i: (0, j, 0, 0)),
                pl.BlockSpec((2, 1, LANES, S5_SW), lambda s, j, i: (0, j, 0, 0)),
                pl.BlockSpec((2, 1, S5_SW, LANES), lambda s, j, i: (0, j, 0, 0)),
                pl.BlockSpec((2, 1, S5_SW, LANES), lambda s, j, i: (0, j, 0, 0)),
                pl.BlockSpec((1, S5_ROWS, S5_SW), lambda s, j, i: (j, 0, 0)),
                pl.BlockSpec((1, S5_ROWS, S5_SW), lambda s, j, i: (j, 0, 0))]
    args = [u4, u4, bre, bim, cre, cim, are, aim]
    if has_h0:
        in_specs += [pl.BlockSpec((1, S5_ROWS, S5_SW), lambda s, j, i: (j, 0, 0))] * 2
        args += list(h0)
    fin = jax.ShapeDtypeStruct((nset, S5_NBLK, S5_ROWS, S5_SW), F32)
    fin_spec = pl.BlockSpec((1, 1, S5_ROWS, S5_SW), lambda s, j, i: (s, j, 0, 0))
    return pl.pallas_call(
        functools.partial(_s5_kernel, has_h0=has_h0),
        grid=(nset, S5_NBLK, nt),
        in_specs=in_specs,
        out_specs=[pl.BlockSpec(ublk, lambda s, j, i: (s, 0, i, j)),
                   pl.BlockSpec(ublk, lambda s, j, i: (s, 0, nt - 1 - i, j)),
                   fin_spec, fin_spec],
        out_shape=[jax.ShapeDtypeStruct(u4.shape, F32), jax.ShapeDtypeStruct(u4.shape, F32), fin, fin],
        scratch_shapes=[pltpu.VMEM((S5_CB, S5_ROWS * S5_T, LANES), F32), pltpu.VMEM((S5_CB, S5_ROWS * S5_T, LANES), F32),
                        pltpu.VMEM((S5_CB, S5_ROWS, LANES), F32), pltpu.VMEM((S5_CB, S5_ROWS, LANES), F32)],
        compiler_params=_cparams("parallel", "parallel", "arbitrary"),
        name="s5_scan_lat" if has_h0 else "s5_scan_ctx",
    )(*args)


def _s5_matrices(lam_re, lam_im, log_step, b_re, b_im, c_re, c_im):
    step = jnp.exp(log_step)[..., None]
    mag = jnp.exp(lam_re * step)
    ab_re, ab_im = mag * jnp.cos(lam_im * step), mag * jnp.sin(lam_im * step)
    den = lam_re * lam_re + lam_im * lam_im
    f_re = ((ab_re - 1) * lam_re + ab_im * lam_im) / den
    f_im = (ab_im * lam_re - (ab_re - 1) * lam_im) / den
    bb_re = f_re[..., None] * b_re - f_im[..., None] * b_im
    bb_im = f_re[..., None] * b_im + f_im[..., None] * b_re
    eye = jnp.eye(S5_GB, dtype=F32)

    def in_blocks(bb):
        bb = bb.reshape(2, S5_NBLK, S5_GB, S5_STATE, S5_CH)
        return jnp.einsum("djgpc,gh->djgchp", bb, eye).reshape(2, S5_NBLK, LANES, S5_SW).astype(BF16)

    def out_blocks(cc):
        cc = cc.reshape(2, S5_NBLK, S5_GB, S5_CH, S5_STATE)
        return jnp.einsum("djgcp,gh->djgphc", cc, eye).reshape(2, S5_NBLK, S5_SW, LANES).astype(BF16)

    def decay_rows(ab):
        ab = ab.reshape(2, S5_NBLK, S5_SW).transpose(1, 0, 2)
        return jnp.repeat(ab, S5_SEQS, axis=1)

    return (in_blocks(bb_re), in_blocks(bb_im), out_blocks(c_re), out_blocks(c_im),
            decay_rows(ab_re), decay_rows(ab_im))


def _s5_glu_kernel(u_ref, yf_ref, yb_ref, d_ref, w_ref, b_ref, o_ref):
    y = jax.nn.gelu(d_ref[...] * u_ref[...] + yf_ref[...] + yb_ref[...])
    gate = jnp.dot(y.astype(BF16), w_ref[...], preferred_element_type=F32) + b_ref[...]
    o_ref[...] = (y * jax.nn.sigmoid(gate)).astype(o_ref.dtype)


def _s5_glu(u, yf, yb, d_skip, glu_w, glu_b):
    tm = 256
    row = pl.BlockSpec((tm, D_S5), lambda i: (i, 0))
    vec = pl.BlockSpec((1, D_S5), lambda i: (0, 0))
    return pl.pallas_call(
        _s5_glu_kernel,
        grid=(M_ALL // tm,),
        in_specs=[row, row, row, vec, pl.BlockSpec((D_S5, D_S5), lambda i: (0, 0)), vec],
        out_specs=row,
        out_shape=jax.ShapeDtypeStruct((M_ALL, D_S5), BF16),
        compiler_params=_cparams("parallel"),
        name="s5_glu",
    )(u, yf, yb, d_skip.reshape(1, D_S5), glu_w, glu_b.reshape(1, D_S5))


def _mixer_ssd(z, xbc, dt_raw, p, h0_ssd):
    w, b = p["ssd_conv_w"], p["ssd_conv_b"]
    xs = _conv_silu(xbc, w, b, 0, D_SSD)
    bm = _conv_silu(xbc, w, b, D_SSD, SSD_GN)
    cm = _conv_silu(xbc, w, b, D_SSD + SSD_GN, SSD_GN)
    ys, fins = [], []
    for d in range(2):
        y, fin = _ssd_scan(xs, bm, cm, dt_raw, p["ssd_dt_bias"], p["ssd_a_log"],
                           _ssd_state_to_pairs(h0_ssd[:, d]), d)
        ys.append(y)
        fins.append(_ssd_state_from_pairs(fin[:BATCH]))
    o = _ssd_out(ys[0], ys[1], xs, z, p["ssd_d"], p["ssd_norm_g"])
    return o, jnp.stack(fins, axis=1)


def _mixer_attn(q, k, v, p, k_ctx, v_ctx):
    o_c, kn_c = _ctx_attn(q, k, v, p["na_q_g"], p["na_k_g"])
    o_l = _lat_attn(q, k, v, k_ctx.reshape(DEC_BATCH, PAST_LEN, D_NA), v_ctx.reshape(DEC_BATCH, PAST_LEN, D_NA),
                    _na_bias_table(p["na_rpb"]), p["na_q_g"], p["na_k_g"])
    return jnp.concatenate([o_c, o_l], axis=0), kn_c


def _mixer_s5(u, p, h0_s5):
    mats = _s5_matrices(p["s5_lam_re"], p["s5_lam_im"], p["s5_log_step"], p["s5_b_re"], p["s5_b_im"],
                        p["s5_c_re"], p["s5_c_im"])
    u_c = u[:M_CTX].reshape(BATCH // S5_SEQS, S5_SEQS, SEQ, D_S5)
    u_l = u[M_CTX:].reshape(DEC_BATCH // S5_SEQS, S5_SEQS, DEC_SEQ, D_S5)

    def h0_rows(part):
        part = part.transpose(1, 0, 2, 3).reshape(S5_ROWS, S5_NBLK, S5_SW)
        return part.transpose(1, 0, 2)

    yf_c, yb_c, fre, fim = _s5_scan(u_c, mats, None)
    yf_l, yb_l, _, _ = _s5_scan(u_l, mats, (h0_rows(h0_s5[:, :, 0]), h0_rows(h0_s5[:, :, 1])))

    def join(a, b):
        return jnp.concatenate([a.reshape(M_CTX, D_S5), b.reshape(M_LAT, D_S5)], axis=0)

    def fin_states(f):
        f = f.reshape(BATCH // S5_SEQS, S5_NBLK, 2, S5_SEQS, S5_SW).transpose(0, 3, 2, 1, 4)
        return f.reshape(BATCH, 2, S5_GROUPS, S5_STATE)

    o = _s5_glu(u, join(yf_c, yf_l), join(yb_c, yb_l), p["s5_d"], p["s5_glu_w"].astype(BF16), p["s5_glu_b"])
    return o, jnp.stack([fin_states(fre), fin_states(fim)], axis=2)


def _layer(x, mod, p, ctx_l):
    bf = lambda w: w.astype(BF16)
    cuts = np.cumsum((0, 3 * D_MODEL, D_SSD, SSD_CONV_CH, 2 * SSD_HEADS, D_NA, D_NA, D_NA, D_S5))
    w_in = p["w_in"]
    h = _norm_mod(x, p["norm1_g"], mod, shift_chunk=0, scale_chunk=1)

    def proj(idx, tn):
        w = bf(w_in[:, cuts[idx]:cuts[idx + 1]])
        return _matmul(h, w, tm=1024, tn=tn, out_dtype=F32, name=f"in_proj{idx}")

    gates = proj(0, 1024)
    z = proj(1, 1024)
    xbc = proj(2, 1024)
    dt_raw = proj(3, 128)
    q = proj(4, 1024)
    k = proj(5, 1024)
    v = proj(6, 1024)
    u = proj(7, 1024)

    k_ctx, v_ctx, h0_ssd, h0_s5 = ctx_l
    o_ssd, st_ssd = _mixer_ssd(z, xbc, dt_raw, p, h0_ssd)
    o_na, kn_c = _mixer_attn(q, k, v, p, k_ctx, v_ctx)
    o_s5, st_s5 = _mixer_s5(u, p, h0_s5)

    merged = _merge(o_ssd, o_na, o_s5, bf(p["w_po_ssd"]), bf(p["w_po_na"]), bf(p["w_po_s5"]), gates)
    x = _matmul(merged, bf(p["w_o"]), tm=1024, tn=1024, out_dtype=F32, res=x, mod=mod, gate_chunk=2, name="w_o")
    h2 = _norm_mod(x, p["norm2_g"], mod, shift_chunk=3, scale_chunk=4)
    a = _matmul(h2, bf(p["mlp_w1"]), tm=1024, tn=1024, out_dtype=BF16, act="relu2", name="mlp1")
    x = _matmul(a, bf(p["mlp_w2"]), tm=1024, tn=1024, tk=2048, out_dtype=F32, res=x, mod=mod, gate_chunk=5,
                name="mlp2")
    cache_shape = (BATCH, SEQ, NA_HEADS, NA_HEAD_DIM)
    return x, (kn_c.reshape(cache_shape), v[:M_CTX].reshape(cache_shape), st_ssd, st_s5)


def kernel(x_prompt, x_sample, c, cache_k, cache_v, state_ssd, state_s5, c_ctx, ada_w, ada_b, norm1_g, norm2_g, w_in, ssd_conv_w, ssd_conv_b, ssd_dt_bias, ssd_a_log, ssd_d, ssd_norm_g, na_q_g, na_k_g, na_rpb, s5_lam_re, s5_lam_im, s5_log_step, s5_b_re, s5_b_im, s5_c_re, s5_c_im, s5_d, s5_glu_w, s5_glu_b, w_po_ssd, w_po_na, w_po_s5, w_o, mlp_w1, mlp_w2):
    params = dict(norm1_g=norm1_g, norm2_g=norm2_g, w_in=w_in, ssd_conv_w=ssd_conv_w, ssd_conv_b=ssd_conv_b,
                  ssd_dt_bias=ssd_dt_bias, ssd_a_log=ssd_a_log, ssd_d=ssd_d, ssd_norm_g=ssd_norm_g,
                  na_q_g=na_q_g, na_k_g=na_k_g, na_rpb=na_rpb, s5_lam_re=s5_lam_re, s5_lam_im=s5_lam_im,
                  s5_log_step=s5_log_step, s5_b_re=s5_b_re, s5_b_im=s5_b_im, s5_c_re=s5_c_re, s5_c_im=s5_c_im,
                  s5_d=s5_d, s5_glu_w=s5_glu_w, s5_glu_b=s5_glu_b, w_po_ssd=w_po_ssd, w_po_na=w_po_na,
                  w_po_s5=w_po_s5, w_o=w_o, mlp_w1=mlp_w1, mlp_w2=mlp_w2)
    cvec = jnp.concatenate([c_ctx[None, :], c, jnp.zeros((MOD_ROWS - 1 - DEC_BATCH, D_MODEL), F32)], axis=0)
    mods = _ada_mod(cvec, ada_w, ada_b)
    x = jnp.concatenate([x_prompt.reshape(M_CTX, D_MODEL), x_sample.reshape(M_LAT, D_MODEL)], axis=0)
    ks, vs, ssds, s5s = [], [], [], []
    for l in range(DEPTH):
        p = {name: val[l] for name, val in params.items()}
        mod = mods[l].reshape(MOD_ROWS, 1, N_MOD * D_MODEL)
        ctx_l = (cache_k[:, l], cache_v[:, l], state_ssd[:, l], state_s5[:, l])
        x, (k_l, v_l, ssd_l, s5_l) = _layer(x, mod, p, ctx_l)
        ks.append(k_l)
        vs.append(v_l)
        ssds.append(ssd_l)
        s5s.append(s5_l)
    y_prompt = x[:M_CTX].reshape(BATCH, SEQ, D_MODEL)
    y_sample = x[M_CTX:].reshape(DEC_BATCH, DEC_SEQ, D_MODEL)
    return (y_prompt, y_sample, jnp.stack(ks, axis=1), jnp.stack(vs, axis=1),
            jnp.stack(ssds, axis=1), jnp.stack(s5s, axis=1))
```

```python
import functools

import jax
import jax.numpy as jnp
import numpy as np
from jax import lax
from jax.experimental import pallas as pl
from jax.experimental.pallas import tpu as pltpu

F32 = jnp.float32
BF16 = jnp.bfloat16

D_MODEL = 4096
BATCH = 16
SEQ = 256
DEPTH = 2
DEC_BATCH = 4
DEC_SEQ = 2048
PAST_LEN = 256
GRID_W = 64
GRID_ROWS = DEC_SEQ // GRID_W
D_SSD = D_MODEL
SSD_HEAD_DIM = 64
SSD_HEADS = D_SSD // SSD_HEAD_DIM
SSD_GROUPS = 8
SSD_STATE = 128
SSD_GN = SSD_GROUPS * SSD_STATE
SSD_CONV_W = 5
SSD_CONV_CH = D_SSD + 2 * SSD_GN
SSD_CHUNK = 128
D_NA = D_MODEL // 2
NA_HEAD_DIM = 128
NA_HEADS = D_NA // NA_HEAD_DIM
NA_WIN_R = 8
NA_WIN_C = 16
D_S5 = D_MODEL // 2
S5_CH = 16
S5_GROUPS = D_S5 // S5_CH
S5_STATE = 64
D_FF = 4 * D_MODEL
N_MOD = 6
EPS = 1e-6
NEG_INF = -1e30

M_CTX = BATCH * SEQ
M_LAT = DEC_BATCH * DEC_SEQ
M_ALL = M_CTX + M_LAT
N_SEQ_ALL = BATCH + DEC_BATCH
MOD_ROWS = 8

LANES = 128
SUBLANES = 8
VMEM_LIMIT = 56 * 1024 * 1024


def _cparams(*sem):
    return pltpu.CompilerParams(dimension_semantics=sem, vmem_limit_bytes=VMEM_LIMIT)


def _mod_row(i, tm):
    n_ctx = M_CTX // tm
    per_seq = DEC_SEQ // tm
    return jnp.where(i < n_ctx, 0, 1 + (i - n_ctx) // per_seq)


def _seq_pos(r, rows):
    n_ctx = M_CTX // rows
    per_ctx = SEQ // rows
    per_lat = DEC_SEQ // rows
    is_lat = r >= n_ctx
    rl = jnp.where(is_lat, r - n_ctx, 0)
    pos = jnp.where(is_lat, rl % per_lat, r % per_ctx)
    first = pos == 0
    last = jnp.where(is_lat, pos == per_lat - 1, pos == per_ctx - 1)
    seq = jnp.where(is_lat, BATCH + rl // per_lat, r // per_ctx)
    return is_lat, first, last, seq


def _silu(x):
    return x * jax.nn.sigmoid(x)


def _ada_kernel(c_ref, w_ref, b_ref, o_ref):
    s = _silu(c_ref[...]).astype(BF16)
    acc = jnp.dot(s, w_ref[0].astype(BF16), preferred_element_type=F32)
    o_ref[0] = acc + b_ref[0]


def _ada_mod(cvec, ada_w, ada_b):
    n = N_MOD * D_MODEL
    tn = 512
    return pl.pallas_call(
        _ada_kernel,
        grid=(DEPTH, n // tn),
        in_specs=[pl.BlockSpec((MOD_ROWS, D_MODEL), lambda l, j: (0, 0)),
                  pl.BlockSpec((1, D_MODEL, tn), lambda l, j: (l, 0, j)),
                  pl.BlockSpec((1, 1, tn), lambda l, j: (l, 0, j))],
        out_specs=pl.BlockSpec((1, MOD_ROWS, tn), lambda l, j: (l, 0, j)),
        out_shape=jax.ShapeDtypeStruct((DEPTH, MOD_ROWS, n), F32),
        compiler_params=_cparams("parallel", "parallel"),
        name="ada_mod",
    )(cvec, ada_w, ada_b.reshape(DEPTH, 1, n))


def _norm_mod_kernel(x_ref, g_ref, sc_ref, sh_ref, o_ref):
    x = x_ref[...]
    ms = jnp.mean(x * x, axis=-1, keepdims=True)
    y = x * lax.rsqrt(ms + EPS) * g_ref[...]
    o_ref[...] = (y * (1.0 + sc_ref[0]) + sh_ref[0]).astype(o_ref.dtype)


def _norm_mod(x, g, mod, shift_chunk, scale_chunk):
    tm = 256
    return pl.pallas_call(
        _norm_mod_kernel,
        grid=(M_ALL // tm,),
        in_specs=[pl.BlockSpec((tm, D_MODEL), lambda i: (i, 0)),
                  pl.BlockSpec((1, D_MODEL), lambda i: (0, 0)),
                  pl.BlockSpec((1, 1, D_MODEL), lambda i: (_mod_row(i, tm), 0, scale_chunk)),
                  pl.BlockSpec((1, 1, D_MODEL), lambda i: (_mod_row(i, tm), 0, shift_chunk))],
        out_specs=pl.BlockSpec((tm, D_MODEL), lambda i: (i, 0)),
        out_shape=jax.ShapeDtypeStruct((M_ALL, D_MODEL), BF16),
        compiler_params=_cparams("parallel"),
        name="norm_mod",
    )(x, g.reshape(1, D_MODEL), mod, mod)


def _mm_kernel(*refs, nk, act, has_res):
    if has_res:
        x_ref, w_ref, res_ref, gt_ref, o_ref = refs[:5]
        scratch = refs[5:]
    else:
        x_ref, w_ref, o_ref = refs[:3]
        scratch = refs[3:]

    def finish(v):
        if act == "relu2":
            r = jnp.maximum(v, 0.0)
            v = r * r
        if has_res:
            v = res_ref[...] + gt_ref[0] * v
        o_ref[...] = v.astype(o_ref.dtype)

    part = jnp.dot(x_ref[...], w_ref[...], preferred_element_type=F32)
    if nk == 1:
        finish(part)
    else:
        acc_ref, = scratch
        k = pl.program_id(2)

        @pl.when(k == 0)
        def _():
            acc_ref[...] = part

        @pl.when(k > 0)
        def _():
            acc_ref[...] += part

        @pl.when(k == nk - 1)
        def _():
            finish(acc_ref[...])


def _matmul(x, w, *, tm, tn, tk=None, out_dtype, act=None, res=None, mod=None, gate_chunk=None, name):
    m, kdim = x.shape
    n = w.shape[1]
    tk = kdim if tk is None else tk
    nk = kdim // tk
    has_res = res is not None
    in_specs = [pl.BlockSpec((tm, tk), lambda i, j, k: (i, k)),
                pl.BlockSpec((tk, tn), lambda i, j, k: (k, j))]
    args = [x, w]
    if has_res:
        nj = n // tn
        in_specs += [pl.BlockSpec((tm, tn), lambda i, j, k: (i, j)),
                     pl.BlockSpec((1, 1, tn), lambda i, j, k: (_mod_row(i, tm), 0, gate_chunk * nj + j))]
        args += [res, mod]
    return pl.pallas_call(
        functools.partial(_mm_kernel, nk=nk, act=act, has_res=has_res),
        grid=(m // tm, n // tn, nk),
        in_specs=in_specs,
        out_specs=pl.BlockSpec((tm, tn), lambda i, j, k: (i, j)),
        out_shape=jax.ShapeDtypeStruct((m, n), out_dtype),
        scratch_shapes=[pltpu.VMEM((tm, tn), F32)] if nk > 1 else [],
        compiler_params=_cparams("parallel", "parallel", "arbitrary"),
        name=name,
    )(*args)


def _merge_kernel(o1_ref, o2_ref, o3_ref, w1_ref, w2_ref, w3_ref, g1_ref, g2_ref, g3_ref, out_ref):
    acc = jax.nn.sigmoid(g1_ref[...]) * jnp.dot(o1_ref[...], w1_ref[...], preferred_element_type=F32)
    acc += jax.nn.sigmoid(g2_ref[...]) * jnp.dot(o2_ref[...], w2_ref[...], preferred_element_type=F32)
    acc += jax.nn.sigmoid(g3_ref[...]) * jnp.dot(o3_ref[...], w3_ref[...], preferred_element_type=F32)
    out_ref[...] = acc.astype(out_ref.dtype)


def _merge(o_ssd, o_na, o_s5, w1, w2, w3, gates):
    tm, tn = 512, 512
    nj = D_MODEL // tn
    return pl.pallas_call(
        _merge_kernel,
        grid=(M_ALL // tm, nj),
        in_specs=[pl.BlockSpec((tm, D_SSD), lambda i, j: (i, 0)),
                  pl.BlockSpec((tm, D_NA), lambda i, j: (i, 0)),
                  pl.BlockSpec((tm, D_S5), lambda i, j: (i, 0)),
                  pl.BlockSpec((D_SSD, tn), lambda i, j: (0, j)),
                  pl.BlockSpec((D_NA, tn), lambda i, j: (0, j)),
                  pl.BlockSpec((D_S5, tn), lambda i, j: (0, j)),
                  pl.BlockSpec((tm, tn), lambda i, j: (i, j)),
                  pl.BlockSpec((tm, tn), lambda i, j: (i, nj + j)),
                  pl.BlockSpec((tm, tn), lambda i, j: (i, 2 * nj + j))],
        out_specs=pl.BlockSpec((tm, tn), lambda i, j: (i, j)),
        out_shape=jax.ShapeDtypeStruct((M_ALL, D_MODEL), BF16),
        compiler_params=_cparams("parallel", "parallel"),
        name="merge",
    )(o_ssd, o_na, o_s5, w1, w2, w3, gates, gates, gates)


CONV_TM = SEQ
CONV_HALO = SUBLANES


def _conv_kernel(prev_ref, x_ref, next_ref, w_ref, b_ref, o_ref):
    _, first, last, _ = _seq_pos(pl.program_id(0), CONV_TM)
    prev = jnp.where(first, 0.0, prev_ref[...])
    nxt = jnp.where(last, 0.0, next_ref[...])
    ext = jnp.concatenate([prev, x_ref[...], nxt], axis=0)
    acc = b_ref[...]
    for k in range(SSD_CONV_W):
        off = CONV_HALO - SSD_CONV_W // 2 + k
        acc = acc + w_ref[k:k + 1, :] * ext[off:off + CONV_TM]
    o_ref[...] = _silu(acc)


def _conv_silu(xbc, w, b, col0, width):
    ct = 512
    joff = col0 // ct
    halo_per = CONV_TM // CONV_HALO
    n_halo = M_ALL // CONV_HALO
    return pl.pallas_call(
        _conv_kernel,
        grid=(M_ALL // CONV_TM, width // ct),
        in_specs=[pl.BlockSpec((CONV_HALO, ct), lambda i, j: (jnp.maximum(i * halo_per - 1, 0), joff + j)),
                  pl.BlockSpec((CONV_TM, ct), lambda i, j: (i, joff + j)),
                  pl.BlockSpec((CONV_HALO, ct), lambda i, j: (jnp.minimum((i + 1) * halo_per, n_halo - 1), joff + j)),
                  pl.BlockSpec((SSD_CONV_W, ct), lambda i, j: (0, joff + j)),
                  pl.BlockSpec((1, ct), lambda i, j: (0, joff + j))],
        out_specs=pl.BlockSpec((CONV_TM, ct), lambda i, j: (i, j)),
        out_shape=jax.ShapeDtypeStruct((M_ALL, width), F32),
        compiler_params=_cparams("parallel", "parallel"),
        name="ssd_conv",
    )(xbc, xbc, xbc, w, b.reshape(1, -1))


SSD_NCH = M_ALL // SSD_CHUNK
SSD_PAIRS = SSD_HEADS // 2
HEADS_PER_GROUP = SSD_HEADS // SSD_GROUPS


def _cumsum_rows(a):
    t = a.shape[0]
    row = lax.broadcasted_iota(jnp.int32, (t, t), 0)
    col = lax.broadcasted_iota(jnp.int32, (t, t), 1)
    tri = (col <= row).astype(BF16)
    hi = a.astype(BF16)
    r1 = a - hi.astype(F32)
    mid = r1.astype(BF16)
    lo = (r1 - mid.astype(F32)).astype(BF16)
    return (jnp.dot(tri, hi, preferred_element_type=F32) + jnp.dot(tri, mid, preferred_element_type=F32)
            + jnp.dot(tri, lo, preferred_element_type=F32))


def _ssd_kernel(x_ref, b_ref, c_ref, dt_ref, dtb_ref, alog_ref, h0_ref, y_ref, fin_ref, st_ref, aux_ref, *, d):
    T = SSD_CHUNK
    i = pl.program_id(0)
    r = i if d == 0 else SSD_NCH - 1 - i
    is_lat, seq_first, seq_last, _ = _seq_pos(r, T)
    first = seq_first if d == 0 else seq_last
    last = seq_last if d == 0 else seq_first

    @pl.when(first & is_lat)
    def _():
        st_ref[...] = h0_ref[0]

    @pl.when(first & jnp.logical_not(is_lat))
    def _():
        st_ref[...] = jnp.zeros_like(st_ref)

    dt = jax.nn.softplus(dt_ref[...] + dtb_ref[...])
    a = dt * (-jnp.exp(alog_ref[...]))
    cs = _cumsum_rows(a)
    tot = cs[T - 1:T, :]
    if d == 0:
        pot = cs
        wexp = jnp.exp(tot - cs)
    else:
        ecs = cs - a
        pot = -ecs
        wexp = jnp.exp(ecs)
    aux_ref[0] = pot.T
    aux_ref[1] = dt.T
    aux_ref[2] = (dt * wexp).T
    aux_ref[3] = jnp.broadcast_to(tot, (T, T)).T

    row = lax.broadcasted_iota(jnp.int32, (T, T), 0)
    col = lax.broadcasted_iota(jnp.int32, (T, T), 1)
    mask = (row >= col) if d == 0 else (col >= row)
    lane_lo = lax.broadcasted_iota(jnp.int32, (T, LANES), 1) < SSD_HEAD_DIM

    for g in range(SSD_GROUPS):
        gs = slice(g * SSD_STATE, (g + 1) * SSD_STATE)
        bg = b_ref[:, gs]
        cg = c_ref[:, gs]
        cb = lax.dot_general(cg.astype(BF16), bg.astype(BF16), (((1,), (1,)), ((), ())),
                             preferred_element_type=F32)
        bgt = bg.T
        for pp in range(HEADS_PER_GROUP // 2):
            pair = g * (HEADS_PER_GROUP // 2) + pp
            ps = slice(pair * LANES, (pair + 1) * LANES)
            xp = x_ref[:, ps].astype(BF16)
            st = st_ref[pair]
            st16 = st.astype(BF16)
            ys, incs, decs = [], [], []
            for e in range(2):
                h = d * SSD_HEADS + pair * 2 + e
                colb = jnp.broadcast_to(pot[:, h:h + 1], (T, T))
                lm = jnp.exp(jnp.where(mask, colb - aux_ref[0, h:h + 1, :], NEG_INF))
                mh = (cb * lm * aux_ref[1, h:h + 1, :]).astype(BF16)
                totrow = aux_ref[3, h:h + 1, :]
                oh = jnp.exp(colb) if d == 0 else jnp.exp(colb + totrow)
                co = (cg * oh).astype(BF16)
                ys.append(jnp.dot(mh, xp, preferred_element_type=F32)
                          + jnp.dot(co, st16, preferred_element_type=F32))
                btw = (bgt * aux_ref[2, h:h + 1, :]).astype(BF16)
                incs.append(jnp.dot(btw, xp, preferred_element_type=F32))
                decs.append(jnp.exp(totrow))
            y_ref[:, ps] = jnp.where(lane_lo, ys[0], ys[1])
            dec = jnp.where(lane_lo[:1], decs[0], decs[1])
            st_ref[pair] = dec * st + jnp.where(lane_lo, incs[0], incs[1])

    @pl.when(last)
    def _():
        fin_ref[0] = st_ref[...]


def _ssd_scan(xs, bm, cm, dt_raw, dt_bias, a_log, h0, d):
    T = SSD_CHUNK

    def rblk(i):
        return i if d == 0 else SSD_NCH - 1 - i

    def seq_of(i):
        return _seq_pos(rblk(i), T)[3]

    return pl.pallas_call(
        functools.partial(_ssd_kernel, d=d),
        grid=(SSD_NCH,),
        in_specs=[pl.BlockSpec((T, D_SSD), lambda i: (rblk(i), 0)),
                  pl.BlockSpec((T, SSD_GN), lambda i: (rblk(i), 0)),
                  pl.BlockSpec((T, SSD_GN), lambda i: (rblk(i), 0)),
                  pl.BlockSpec((T, 2 * SSD_HEADS), lambda i: (rblk(i), 0)),
                  pl.BlockSpec((1, 2 * SSD_HEADS), lambda i: (0, 0)),
                  pl.BlockSpec((1, 2 * SSD_HEADS), lambda i: (0, 0)),
                  pl.BlockSpec((1, SSD_PAIRS, SSD_STATE, LANES),
                               lambda i: (jnp.maximum(seq_of(i) - BATCH, 0), 0, 0, 0))],
        out_specs=[pl.BlockSpec((T, D_SSD), lambda i: (rblk(i), 0)),
                   pl.BlockSpec((1, SSD_PAIRS, SSD_STATE, LANES), lambda i: (seq_of(i), 0, 0, 0))],
        out_shape=[jax.ShapeDtypeStruct((M_ALL, D_SSD), F32),
                   jax.ShapeDtypeStruct((N_SEQ_ALL, SSD_PAIRS, SSD_STATE, LANES), F32)],
        scratch_shapes=[pltpu.VMEM((SSD_PAIRS, SSD_STATE, LANES), F32),
                        pltpu.VMEM((4, T, T), F32)],
        compiler_params=_cparams("arbitrary"),
        name=f"ssd_scan{d}",
    )(xs, bm, cm, dt_raw, dt_bias.reshape(1, -1), a_log.reshape(1, -1), h0)


def _ssd_out_kernel(yf_ref, yb_ref, xs_ref, z_ref, d_ref, g_ref, o_ref):
    y = (yf_ref[...] + yb_ref[...] + d_ref[...] * xs_ref[...]) * _silu(z_ref[...])
    gw = D_SSD // SSD_GROUPS
    for g in range(SSD_GROUPS):
        gs = slice(g * gw, (g + 1) * gw)
        yg = y[:, gs]
        ms = jnp.mean(yg * yg, axis=-1, keepdims=True)
        o_ref[:, gs] = (yg * lax.rsqrt(ms + EPS) * g_ref[:, gs]).astype(o_ref.dtype)


def _ssd_out(yf, yb, xs, z, d_skip, norm_g):
    tm = 256
    row = pl.BlockSpec((tm, D_SSD), lambda i: (i, 0))
    vec = pl.BlockSpec((1, D_SSD), lambda i: (0, 0))
    return pl.pallas_call(
        _ssd_out_kernel,
        grid=(M_ALL // tm,),
        in_specs=[row, row, row, row, vec, vec],
        out_specs=row,
        out_shape=jax.ShapeDtypeStruct((M_ALL, D_SSD), BF16),
        compiler_params=_cparams("parallel"),
        name="ssd_out",
    )(yf, yb, xs, z, jnp.repeat(d_skip, SSD_HEAD_DIM).reshape(1, D_SSD), norm_g.reshape(1, D_SSD))


def _ssd_state_to_pairs(s):
    b = s.shape[0]
    s = s.reshape(b, SSD_PAIRS, 2, SSD_HEAD_DIM, SSD_STATE)
    return s.transpose(0, 1, 4, 2, 3).reshape(b, SSD_PAIRS, SSD_STATE, LANES)


def _ssd_state_from_pairs(s):
    b = s.shape[0]
    s = s.reshape(b, SSD_PAIRS, SSD_STATE, 2, SSD_HEAD_DIM)
    return s.transpose(0, 1, 3, 4, 2).reshape(b, SSD_HEADS, SSD_HEAD_DIM, SSD_STATE)


def _head_rms(x, g):
    return x * lax.rsqrt(jnp.mean(x * x, axis=-1, keepdims=True) + EPS) * g


ATT_SCALE = NA_HEAD_DIM ** -0.5
CTX_HEADS_PER_STEP = 4


def _ctx_attn_kernel(q_ref, k_ref, v_ref, qg_ref, kg_ref, o_ref, kn_ref):
    for hh in range(CTX_HEADS_PER_STEP):
        hs = slice(hh * NA_HEAD_DIM, (hh + 1) * NA_HEAD_DIM)
        qn = _head_rms(q_ref[:, hs], qg_ref[...])
        kn = _head_rms(k_ref[:, hs], kg_ref[...])
        kn_ref[:, hs] = kn
        s = lax.dot_general(qn.astype(BF16), kn.astype(BF16), (((1,), (1,)), ((), ())),
                            preferred_element_type=F32) * ATT_SCALE
        p = jnp.exp(s - jnp.max(s, axis=-1, keepdims=True))
        den = jnp.sum(p, axis=-1, keepdims=True)
        o = jnp.dot(p.astype(BF16), v_ref[:, hs].astype(BF16), preferred_element_type=F32)
        o_ref[:, hs] = (o / den).astype(o_ref.dtype)


def _ctx_attn(q, k, v, q_g, k_g):
    w = CTX_HEADS_PER_STEP * NA_HEAD_DIM
    blk = pl.BlockSpec((SEQ, w), lambda b, j: (b, j))
    vec = pl.BlockSpec((1, NA_HEAD_DIM), lambda b, j: (0, 0))
    return pl.pallas_call(
        _ctx_attn_kernel,
        grid=(BATCH, D_NA // w),
        in_specs=[blk, blk, blk, vec, vec],
        out_specs=[blk, blk],
        out_shape=[jax.ShapeDtypeStruct((M_CTX, D_NA), BF16), jax.ShapeDtypeStruct((M_CTX, D_NA), F32)],
        compiler_params=_cparams("parallel", "parallel"),
        name="ctx_attn",
    )(q, k, v, q_g.reshape(1, -1), k_g.reshape(1, -1))


NA_BAND_ROWS = 8
NA_BANDS = GRID_ROWS // NA_BAND_ROWS
NA_KEY_ROWS = 2 * NA_BAND_ROWS
NA_BAND_Q = NA_BAND_ROWS * GRID_W
NA_BAND_K = NA_KEY_ROWS * GRID_W


def _na_band_key_row0(band):
    return np.clip(band * NA_BAND_ROWS - NA_WIN_R // 2, 0, GRID_ROWS - NA_KEY_ROWS)


def _na_bias_pairs(rpb):
    qc = np.arange(GRID_W)[:, None]
    kc = np.arange(GRID_W)[None, :]
    ws = np.clip(qc - NA_WIN_C // 2, 0, GRID_W - NA_WIN_C)
    col_ok = (kc >= ws) & (kc < ws + NA_WIN_C)
    dc = np.clip(kc - qc + NA_WIN_C - 1, 0, 2 * NA_WIN_C - 2)
    tm = jnp.where(col_ok, rpb[:, :, dc], NEG_INF)
    neg = jnp.full((NA_HEADS, 1, GRID_W, GRID_W), NEG_INF, F32)
    tm = jnp.concatenate([neg, tm, neg], axis=1)
    return jnp.concatenate([tm[:, :-1], tm[:, 1:]], axis=-1)


def _na_fill_bias(bias_ref, tp_ref, band):
    row0 = int(_na_band_key_row0(band))
    lane_lo = lax.broadcasted_iota(jnp.int32, (GRID_W, 2 * GRID_W), 1) < GRID_W
    for qr in range(NA_BAND_ROWS):
        r = band * NA_BAND_ROWS + qr
        rs = min(max(r - NA_WIN_R // 2, 0), GRID_ROWS - NA_WIN_R)
        for m in range(NA_KEY_ROWS // 2):
            kr = row0 + 2 * m
            ok_l = rs <= kr < rs + NA_WIN_R
            ok_r = rs <= kr + 1 < rs + NA_WIN_R
            dst = (slice(qr * GRID_W, (qr + 1) * GRID_W), slice(m * 2 * GRID_W, (m + 1) * 2 * GRID_W))
            if not (ok_l or ok_r):
                bias_ref[dst] = jnp.full((GRID_W, 2 * GRID_W), NEG_INF, F32)
                continue
            blk = tp_ref[0, kr - r + NA_WIN_R]
            if not ok_r:
                blk = jnp.where(lane_lo, blk, NEG_INF)
            elif not ok_l:
                blk = jnp.where(lane_lo, NEG_INF, blk)
            bias_ref[dst] = blk


def _lat_attn_kernel(q_ref, k_ref, v_ref, kc_ref, vc_ref, tp_ref, qg_ref, kg_ref, o_ref, bias_ref):
    band = pl.program_id(1)
    for n in range(NA_BANDS):
        @pl.when((pl.program_id(2) == 0) & (band == n))
        def _(n=n):
            _na_fill_bias(bias_ref, tp_ref, n)

    row0 = jnp.clip(band * NA_BAND_ROWS - NA_WIN_R // 2, 0, GRID_ROWS - NA_KEY_ROWS)
    k0 = pl.multiple_of(row0 * GRID_W, NA_WIN_R // 2 * GRID_W)
    qn = _head_rms(q_ref[...], qg_ref[...]).astype(BF16)
    kn = _head_rms(k_ref[pl.ds(k0, NA_BAND_K), :], kg_ref[...]).astype(BF16)
    vw = v_ref[pl.ds(k0, NA_BAND_K), :].astype(BF16)
    nt = (((1,), (1,)), ((), ()))
    s_loc = lax.dot_general(qn, kn, nt, preferred_element_type=F32) * ATT_SCALE + bias_ref[...]
    s_ctx = lax.dot_general(qn, kc_ref[0].astype(BF16), nt, preferred_element_type=F32) * ATT_SCALE
    m = jnp.maximum(jnp.max(s_loc, axis=-1, keepdims=True), jnp.max(s_ctx, axis=-1, keepdims=True))
    p_loc = jnp.exp(s_loc - m)
    p_ctx = jnp.exp(s_ctx - m)
    den = jnp.sum(p_loc, axis=-1, keepdims=True) + jnp.sum(p_ctx, axis=-1, keepdims=True)
    o = (jnp.dot(p_loc.astype(BF16), vw, preferred_element_type=F32)
         + jnp.dot(p_ctx.astype(BF16), vc_ref[0].astype(BF16), preferred_element_type=F32))
    o_ref[...] = (o / den).astype(o_ref.dtype)


def _lat_attn(q, k, v, k_ctx, v_ctx, bias_pairs, q_g, k_g):
    q_blk0 = M_CTX // NA_BAND_Q
    seq_blk0 = M_CTX // DEC_SEQ
    hd = NA_HEAD_DIM
    vec = pl.BlockSpec((1, hd), lambda h, n, b: (0, 0))
    return pl.pallas_call(
        _lat_attn_kernel,
        grid=(NA_HEADS, NA_BANDS, DEC_BATCH),
        in_specs=[pl.BlockSpec((NA_BAND_Q, hd), lambda h, n, b: (q_blk0 + b * NA_BANDS + n, h)),
                  pl.BlockSpec((DEC_SEQ, hd), lambda h, n, b: (seq_blk0 + b, h)),
                  pl.BlockSpec((DEC_SEQ, hd), lambda h, n, b: (seq_blk0 + b, h)),
                  pl.BlockSpec((1, PAST_LEN, hd), lambda h, n, b: (b, 0, h)),
                  pl.BlockSpec((1, PAST_LEN, hd), lambda h, n, b: (b, 0, h)),
                  pl.BlockSpec((1, 2 * NA_WIN_R, GRID_W, 2 * GRID_W), lambda h, n, b: (h, 0, 0, 0)),
                  vec, vec],
        out_specs=pl.BlockSpec((NA_BAND_Q, hd), lambda h, n, b: (b * NA_BANDS + n, h)),
        out_shape=jax.ShapeDtypeStruct((M_LAT, D_NA), BF16),
        scratch_shapes=[pltpu.VMEM((NA_BAND_Q, NA_BAND_K), F32)],
        compiler_params=_cparams("parallel", "parallel", "arbitrary"),
        name="lat_attn",
    )(q, k, v, k_ctx, v_ctx, bias_pairs, q_g.reshape(1, -1), k_g.reshape(1, -1))


S5_SEQS = 4
S5_ROWS = 2 * S5_SEQS
S5_GB = LANES // S5_CH
S5_NBLK = D_S5 // LANES
S5_SW = S5_GB * S5_STATE
S5_CB = S5_SW // LANES


def _s5_kernel(*refs, has_h0, tsteps):
    if has_h0:
        (u_ref, b_ref, c_ref, are_ref, aim_ref, h0re_ref, h0im_ref,
         y_ref, yrev_ref, fre_ref, fim_ref, bu_ref, hre_ref, him_ref) = refs
    else:
        u_ref, b_ref, c_ref, are_ref, aim_ref, y_ref, yrev_ref, fre_ref, fim_ref, bu_ref, hre_ref, him_ref = refs
    T = tsteps
    rows = T * S5_ROWS

    @pl.when(pl.program_id(2) == 0)
    def _():
        if has_h0:
            hre_ref[...] = h0re_ref[0]
            him_ref[...] = h0im_ref[0]
        else:
            hre_ref[...] = jnp.zeros_like(hre_ref)
            him_ref[...] = jnp.zeros_like(him_ref)

    is_fwd = lax.broadcasted_iota(jnp.int32, (rows, LANES), 0) % S5_ROWS < S5_SEQS
    u = u_ref[0].reshape(rows, LANES)
    lhs = jnp.concatenate([jnp.where(is_fwd, u, 0.0), jnp.where(is_fwd, 0.0, u)], axis=1).astype(BF16)
    bu_ref[...] = jnp.dot(lhs, b_ref[0], preferred_element_type=F32)

    cols = [slice(cb * LANES, (cb + 1) * LANES) for cb in range(S5_CB)]
    a_re = [are_ref[0, :, cs] for cs in cols]
    a_im = [aim_ref[0, :, cs] for cs in cols]

    def step(t, carry):
        h_re, h_im = carry
        rs = pl.ds(pl.multiple_of(t * S5_ROWS, S5_ROWS), S5_ROWS)
        n_re, n_im = [], []
        for cb in range(S5_CB):
            cre = cols[cb]
            cim = slice(S5_SW + cb * LANES, S5_SW + (cb + 1) * LANES)
            r = a_re[cb] * h_re[cb] - a_im[cb] * h_im[cb] + bu_ref[rs, cre]
            m = a_re[cb] * h_im[cb] + a_im[cb] * h_re[cb] + bu_ref[rs, cim]
            bu_ref[rs, cre] = r
            bu_ref[rs, cim] = m
            n_re.append(r)
            n_im.append(m)
        return tuple(n_re), tuple(n_im)

    init = (tuple(hre_ref[:, cs] for cs in cols), tuple(him_ref[:, cs] for cs in cols))
    h_re, h_im = lax.fori_loop(0, T, step, init, unroll=4)
    for cb in range(S5_CB):
        hre_ref[:, cols[cb]] = h_re[cb]
        him_ref[:, cols[cb]] = h_im[cb]
        fre_ref[0, 0, :, cols[cb]] = h_re[cb]
        fim_ref[0, 0, :, cols[cb]] = h_im[cb]

    y2 = jnp.dot(bu_ref[...].astype(BF16), c_ref[0], preferred_element_type=F32)
    y = jnp.where(is_fwd, y2[:, :LANES], y2[:, LANES:]).reshape(T, S5_ROWS, LANES)
    y_ref[0] = y
    for t in range(T):
        yrev_ref[0, T - 1 - t] = y[t]


def _s5_scan(u8, mats, h0, tsteps):
    nset, L, _, _ = u8.shape
    nt = L // tsteps
    bcat, ccat, are, aim = mats
    has_h0 = h0 is not None
    ublk = pl.BlockSpec((1, tsteps, S5_ROWS, LANES), lambda s, j, i: (s, i, 0, j))
    rowblk = pl.BlockSpec((1, S5_ROWS, S5_SW), lambda s, j, i: (j, 0, 0))
    in_specs = [ublk,
                pl.BlockSpec((1, 2 * LANES, 2 * S5_SW), lambda s, j, i: (j, 0, 0)),
                pl.BlockSpec((1, 2 * S5_SW, 2 * LANES), lambda s, j, i: (j, 0, 0)),
                rowblk, rowblk]
    args = [u8, bcat, ccat, are, aim]
    if has_h0:
        in_specs += [rowblk, rowblk]
        args += list(h0)
    fin = jax.ShapeDtypeStruct((nset, S5_NBLK, S5_ROWS, S5_SW), F32)
    fin_spec = pl.BlockSpec((1, 1, S5_ROWS, S5_SW), lambda s, j, i: (s, j, 0, 0))
    return pl.pallas_call(
        functools.partial(_s5_kernel, has_h0=has_h0, tsteps=tsteps),
        grid=(nset, S5_NBLK, nt),
        in_specs=in_specs,
        out_specs=[ublk, pl.BlockSpec((1, tsteps, S5_ROWS, LANES), lambda s, j, i: (s, nt - 1 - i, 0, j)),
                   fin_spec, fin_spec],
        out_shape=[jax.ShapeDtypeStruct(u8.shape, F32), jax.ShapeDtypeStruct(u8.shape, F32), fin, fin],
        scratch_shapes=[pltpu.VMEM((tsteps * S5_ROWS, 2 * S5_SW), F32),
                        pltpu.VMEM((S5_ROWS, S5_SW), F32), pltpu.VMEM((S5_ROWS, S5_SW), F32)],
        compiler_params=_cparams("parallel", "parallel", "arbitrary"),
        name="s5_scan_lat" if has_h0 else "s5_scan_ctx",
    )(*args)


def _s5_matrices(lam_re, lam_im, log_step, b_re, b_im, c_re, c_im):
    step = jnp.exp(log_step)[..., None]
    mag = jnp.exp(lam_re * step)
    ab_re, ab_im = mag * jnp.cos(lam_im * step), mag * jnp.sin(lam_im * step)
    den = lam_re * lam_re + lam_im * lam_im
    f_re = ((ab_re - 1) * lam_re + ab_im * lam_im) / den
    f_im = (ab_im * lam_re - (ab_re - 1) * lam_im) / den
    bb_re = f_re[..., None] * b_re - f_im[..., None] * b_im
    bb_im = f_re[..., None] * b_im + f_im[..., None] * b_re
    eye = jnp.eye(S5_GB, dtype=F32)

    def in_blocks(bb):
        bb = bb.reshape(2, S5_NBLK, S5_GB, S5_STATE, S5_CH)
        return jnp.einsum("djgpc,gh->djgchp", bb, eye).reshape(2, S5_NBLK, LANES, S5_SW)

    def out_blocks(cc):
        cc = cc.reshape(2, S5_NBLK, S5_GB, S5_CH, S5_STATE)
        return jnp.einsum("djgcp,gh->djgphc", cc, eye).reshape(2, S5_NBLK, S5_SW, LANES)

    def decay_rows(ab):
        ab = ab.reshape(2, S5_NBLK, S5_SW).transpose(1, 0, 2)
        return jnp.repeat(ab, S5_SEQS, axis=1)

    bcat = jnp.concatenate([in_blocks(bb_re), in_blocks(bb_im)], axis=-1)
    bcat = bcat.transpose(1, 0, 2, 3).reshape(S5_NBLK, 2 * LANES, 2 * S5_SW).astype(BF16)
    ccat = jnp.concatenate([out_blocks(c_re), -out_blocks(c_im)], axis=-2)
    ccat = ccat.transpose(1, 2, 0, 3).reshape(S5_NBLK, 2 * S5_SW, 2 * LANES).astype(BF16)
    return bcat, ccat, decay_rows(ab_re), decay_rows(ab_im)


def _s5_glu_kernel(u_ref, yf_ref, yb_ref, d_ref, w_ref, b_ref, o_ref):
    y = jax.nn.gelu(d_ref[...] * u_ref[...] + yf_ref[...] + yb_ref[...])
    gate = jnp.dot(y.astype(BF16), w_ref[...], preferred_element_type=F32) + b_ref[...]
    o_ref[...] = (y * jax.nn.sigmoid(gate)).astype(o_ref.dtype)


def _s5_glu(u, yf, yb, d_skip, glu_w, glu_b):
    tm = 256
    row = pl.BlockSpec((tm, D_S5), lambda i: (i, 0))
    vec = pl.BlockSpec((1, D_S5), lambda i: (0, 0))
    return pl.pallas_call(
        _s5_glu_kernel,
        grid=(M_ALL // tm,),
        in_specs=[row, row, row, vec, pl.BlockSpec((D_S5, D_S5), lambda i: (0, 0)), vec],
        out_specs=row,
        out_shape=jax.ShapeDtypeStruct((M_ALL, D_S5), BF16),
        compiler_params=_cparams("parallel"),
        name="s5_glu",
    )(u, yf, yb, d_skip.reshape(1, D_S5), glu_w, glu_b.reshape(1, D_S5))


def _mixer_ssd(z, xbc, dt_raw, p, h0_ssd):
    w, b = p["ssd_conv_w"], p["ssd_conv_b"]
    xs = _conv_silu(xbc, w, b, 0, D_SSD)
    bm = _conv_silu(xbc, w, b, D_SSD, SSD_GN)
    cm = _conv_silu(xbc, w, b, D_SSD + SSD_GN, SSD_GN)
    ys, fins = [], []
    for d in range(2):
        y, fin = _ssd_scan(xs, bm, cm, dt_raw, p["ssd_dt_bias"], p["ssd_a_log"],
                           _ssd_state_to_pairs(h0_ssd[:, d]), d)
        ys.append(y)
        fins.append(_ssd_state_from_pairs(fin[:BATCH]))
    o = _ssd_out(ys[0], ys[1], xs, z, p["ssd_d"], p["ssd_norm_g"])
    return o, jnp.stack(fins, axis=1)


def _mixer_attn(q, k, v, p, k_ctx, v_ctx):
    o_c, kn_c = _ctx_attn(q, k, v, p["na_q_g"], p["na_k_g"])
    o_l = _lat_attn(q, k, v, k_ctx.reshape(DEC_BATCH, PAST_LEN, D_NA), v_ctx.reshape(DEC_BATCH, PAST_LEN, D_NA),
                    _na_bias_pairs(p["na_rpb"]), p["na_q_g"], p["na_k_g"])
    return jnp.concatenate([o_c, o_l], axis=0), kn_c


def _mixer_s5(u, p, h0_s5):
    mats = _s5_matrices(p["s5_lam_re"], p["s5_lam_im"], p["s5_log_step"], p["s5_b_re"], p["s5_b_im"],
                        p["s5_c_re"], p["s5_c_im"])

    def time_major(part, nset, L):
        fwd = part.reshape(nset, S5_SEQS, L, D_S5).transpose(0, 2, 1, 3)
        return jnp.concatenate([fwd, jnp.flip(fwd, axis=1)], axis=2)

    def seq_major(y8, rows):
        nset, L = y8.shape[:2]
        return y8[:, :, rows].transpose(0, 2, 1, 3).reshape(nset * S5_SEQS * L, D_S5)

    def h0_rows(part):
        part = part.transpose(1, 0, 2, 3).reshape(S5_ROWS, S5_NBLK, S5_SW)
        return part.transpose(1, 0, 2)

    y_c, yrev_c, fre, fim = _s5_scan(time_major(u[:M_CTX], BATCH // S5_SEQS, SEQ), mats, None, SEQ)
    y_l, yrev_l, _, _ = _s5_scan(time_major(u[M_CTX:], DEC_BATCH // S5_SEQS, DEC_SEQ), mats,
                                 (h0_rows(h0_s5[:, :, 0]), h0_rows(h0_s5[:, :, 1])), SEQ)
    fwd_rows, bwd_rows = slice(0, S5_SEQS), slice(S5_SEQS, S5_ROWS)
    yf_c, yb_c = seq_major(y_c, fwd_rows), seq_major(yrev_c, bwd_rows)
    yf_l, yb_l = seq_major(y_l, fwd_rows), seq_major(yrev_l, bwd_rows)

    def fin_states(f):
        f = f.reshape(BATCH // S5_SEQS, S5_NBLK, 2, S5_SEQS, S5_SW).transpose(0, 3, 2, 1, 4)
        return f.reshape(BATCH, 2, S5_GROUPS, S5_STATE)

    o = _s5_glu(u, jnp.concatenate([yf_c, yf_l], axis=0), jnp.concatenate([yb_c, yb_l], axis=0),
                p["s5_d"], p["s5_glu_w"].astype(BF16), p["s5_glu_b"])
    return o, jnp.stack([fin_states(fre), fin_states(fim)], axis=2)


def _layer(x, mod, p, ctx_l):
    bf = lambda w: w.astype(BF16)
    cuts = np.cumsum((0, 3 * D_MODEL, D_SSD, SSD_CONV_CH, 2 * SSD_HEADS, D_NA, D_NA, D_NA, D_S5))
    w_in = p["w_in"]
    h = _norm_mod(x, p["norm1_g"], mod, shift_chunk=0, scale_chunk=1)

    def proj(idx, tn):
        w = bf(w_in[:, cuts[idx]:cuts[idx + 1]])
        return _matmul(h, w, tm=1024, tn=tn, out_dtype=F32, name=f"in_proj{idx}")

    gates = proj(0, 1024)
    z = proj(1, 1024)
    xbc = proj(2, 1024)
    dt_raw = proj(3, 128)
    q = proj(4, 1024)
    k = proj(5, 1024)
    v = proj(6, 1024)
    u = proj(7, 1024)

    k_ctx, v_ctx, h0_ssd, h0_s5 = ctx_l
    o_ssd, st_ssd = _mixer_ssd(z, xbc, dt_raw, p, h0_ssd)
    o_na, kn_c = _mixer_attn(q, k, v, p, k_ctx, v_ctx)
    o_s5, st_s5 = _mixer_s5(u, p, h0_s5)

    merged = _merge(o_ssd, o_na, o_s5, bf(p["w_po_ssd"]), bf(p["w_po_na"]), bf(p["w_po_s5"]), gates)
    x = _matmul(merged, bf(p["w_o"]), tm=1024, tn=1024, out_dtype=F32, res=x, mod=mod, gate_chunk=2, name="w_o")
    h2 = _norm_mod(x, p["norm2_g"], mod, shift_chunk=3, scale_chunk=4)
    a = _matmul(h2, bf(p["mlp_w1"]), tm=1024, tn=1024, out_dtype=BF16, act="relu2", name="mlp1")
    x = _matmul(a, bf(p["mlp_w2"]), tm=1024, tn=1024, tk=2048, out_dtype=F32, res=x, mod=mod, gate_chunk=5,
                name="mlp2")
    cache_shape = (BATCH, SEQ, NA_HEADS, NA_HEAD_DIM)
    return x, (kn_c.reshape(cache_shape), v[:M_CTX].reshape(cache_shape), st_ssd, st_s5)


def kernel(x_prompt, x_sample, c, cache_k, cache_v, state_ssd, state_s5, c_ctx, ada_w, ada_b, norm1_g, norm2_g, w_in, ssd_conv_w, ssd_conv_b, ssd_dt_bias, ssd_a_log, ssd_d, ssd_norm_g, na_q_g, na_k_g, na_rpb, s5_lam_re, s5_lam_im, s5_log_step, s5_b_re, s5_b_im, s5_c_re, s5_c_im, s5_d, s5_glu_w, s5_glu_b, w_po_ssd, w_po_na, w_po_s5, w_o, mlp_w1, mlp_w2):
    params = dict(norm1_g=norm1_g, norm2_g=norm2_g, w_in=w_in, ssd_conv_w=ssd_conv_w, ssd_conv_b=ssd_conv_b,
                  ssd_dt_bias=ssd_dt_bias, ssd_a_log=ssd_a_log, ssd_d=ssd_d, ssd_norm_g=ssd_norm_g,
                  na_q_g=na_q_g, na_k_g=na_k_g, na_rpb=na_rpb, s5_lam_re=s5_lam_re, s5_lam_im=s5_lam_im,
                  s5_log_step=s5_log_step, s5_b_re=s5_b_re, s5_b_im=s5_b_im, s5_c_re=s5_c_re, s5_c_im=s5_c_im,
                  s5_d=s5_d, s5_glu_w=s5_glu_w, s5_glu_b=s5_glu_b, w_po_ssd=w_po_ssd, w_po_na=w_po_na,
                  w_po_s5=w_po_s5, w_o=w_o, mlp_w1=mlp_w1, mlp_w2=mlp_w2)
    cvec = jnp.concatenate([c_ctx[None, :], c, jnp.zeros((MOD_ROWS - 1 - DEC_BATCH, D_MODEL), F32)], axis=0)
    mods = _ada_mod(cvec, ada_w, ada_b)
    x = jnp.concatenate([x_prompt.reshape(M_CTX, D_MODEL), x_sample.reshape(M_LAT, D_MODEL)], axis=0)
    ks, vs, ssds, s5s = [], [], [], []
    for l in range(DEPTH):
        p = {name: val[l] for name, val in params.items()}
        mod = mods[l].reshape(MOD_ROWS, 1, N_MOD * D_MODEL)
        ctx_l = (cache_k[:, l], cache_v[:, l], state_ssd[:, l], state_s5[:, l])
        x, (k_l, v_l, ssd_l, s5_l) = _layer(x, mod, p, ctx_l)
        ks.append(k_l)
        vs.append(v_l)
        ssds.append(ssd_l)
        s5s.append(s5_l)
    y_prompt = x[:M_CTX].reshape(BATCH, SEQ, D_MODEL)
    y_sample = x[M_CTX:].reshape(DEC_BATCH, DEC_SEQ, D_MODEL)
    return (y_prompt, y_sample, jnp.stack(ks, axis=1), jnp.stack(vs, axis=1),
            jnp.stack(ssds, axis=1), jnp.stack(s5s, axis=1))
```

```python
import functools

import jax
import jax.numpy as jnp
import numpy as np
from jax import lax
from jax.experimental import pallas as pl
from jax.experimental.pallas import tpu as pltpu

F32 = jnp.float32
BF16 = jnp.bfloat16

D_MODEL = 4096
BATCH = 16
SEQ = 256
DEPTH = 2
DEC_BATCH = 4
DEC_SEQ = 2048
PAST_LEN = 256
GRID_W = 64
GRID_ROWS = DEC_SEQ // GRID_W
D_SSD = D_MODEL
SSD_HEAD_DIM = 64
SSD_HEADS = D_SSD // SSD_HEAD_DIM
SSD_GROUPS = 8
SSD_STATE = 128
SSD_GN = SSD_GROUPS * SSD_STATE
SSD_CONV_W = 5
SSD_CONV_CH = D_SSD + 2 * SSD_GN
SSD_CHUNK = 128
D_NA = D_MODEL // 2
NA_HEAD_DIM = 128
NA_HEADS = D_NA // NA_HEAD_DIM
NA_WIN_R = 8
NA_WIN_C = 16
D_S5 = D_MODEL // 2
S5_CH = 16
S5_GROUPS = D_S5 // S5_CH
S5_STATE = 64
D_FF = 4 * D_MODEL
N_MOD = 6
EPS = 1e-6
NEG_INF = -1e30

M_CTX = BATCH * SEQ
M_LAT = DEC_BATCH * DEC_SEQ
M_ALL = M_CTX + M_LAT
N_SEQ_ALL = BATCH + DEC_BATCH
MOD_ROWS = 8

LANES = 128
SUBLANES = 8
VMEM_LIMIT = 56 * 1024 * 1024


def _cparams(*sem):
    return pltpu.CompilerParams(dimension_semantics=sem, vmem_limit_bytes=VMEM_LIMIT)


def _mod_row(i, tm):
    n_ctx = M_CTX // tm
    per_seq = DEC_SEQ // tm
    return jnp.where(i < n_ctx, 0, 1 + (i - n_ctx) // per_seq)


def _seq_pos(r, rows):
    n_ctx = M_CTX // rows
    per_ctx = SEQ // rows
    per_lat = DEC_SEQ // rows
    is_lat = r >= n_ctx
    rl = jnp.where(is_lat, r - n_ctx, 0)
    pos = jnp.where(is_lat, rl % per_lat, r % per_ctx)
    first = pos == 0
    last = jnp.where(is_lat, pos == per_lat - 1, pos == per_ctx - 1)
    seq = jnp.where(is_lat, BATCH + rl // per_lat, r // per_ctx)
    return is_lat, first, last, seq


def _silu(x):
    return x * jax.nn.sigmoid(x)


def _ada_kernel(c_ref, w_ref, b_ref, o_ref):
    s = _silu(c_ref[...]).astype(BF16)
    acc = jnp.dot(s, w_ref[0].astype(BF16), preferred_element_type=F32)
    o_ref[0] = acc + b_ref[0]


def _ada_mod(cvec, ada_w, ada_b):
    n = N_MOD * D_MODEL
    tn = 512
    return pl.pallas_call(
        _ada_kernel,
        grid=(DEPTH, n // tn),
        in_specs=[pl.BlockSpec((MOD_ROWS, D_MODEL), lambda l, j: (0, 0)),
                  pl.BlockSpec((1, D_MODEL, tn), lambda l, j: (l, 0, j)),
                  pl.BlockSpec((1, 1, tn), lambda l, j: (l, 0, j))],
        out_specs=pl.BlockSpec((1, MOD_ROWS, tn), lambda l, j: (l, 0, j)),
        out_shape=jax.ShapeDtypeStruct((DEPTH, MOD_ROWS, n), F32),
        compiler_params=_cparams("parallel", "parallel"),
        name="ada_mod",
    )(cvec, ada_w, ada_b.reshape(DEPTH, 1, n))


def _norm_mod_kernel(x_ref, g_ref, sc_ref, sh_ref, o_ref):
    x = x_ref[...]
    ms = jnp.mean(x * x, axis=-1, keepdims=True)
    y = x * lax.rsqrt(ms + EPS) * g_ref[...]
    o_ref[...] = (y * (1.0 + sc_ref[0]) + sh_ref[0]).astype(o_ref.dtype)


def _norm_mod(x, g, mod, shift_chunk, scale_chunk):
    tm = 256
    return pl.pallas_call(
        _norm_mod_kernel,
        grid=(M_ALL // tm,),
        in_specs=[pl.BlockSpec((tm, D_MODEL), lambda i: (i, 0)),
                  pl.BlockSpec((1, D_MODEL), lambda i: (0, 0)),
                  pl.BlockSpec((1, 1, D_MODEL), lambda i: (_mod_row(i, tm), 0, scale_chunk)),
                  pl.BlockSpec((1, 1, D_MODEL), lambda i: (_mod_row(i, tm), 0, shift_chunk))],
        out_specs=pl.BlockSpec((tm, D_MODEL), lambda i: (i, 0)),
        out_shape=jax.ShapeDtypeStruct((M_ALL, D_MODEL), BF16),
        compiler_params=_cparams("parallel"),
        name="norm_mod",
    )(x, g.reshape(1, D_MODEL), mod, mod)


def _mm_kernel(*refs, nk, act, has_res):
    if has_res:
        x_ref, w_ref, res_ref, gt_ref, o_ref = refs[:5]
        scratch = refs[5:]
    else:
        x_ref, w_ref, o_ref = refs[:3]
        scratch = refs[3:]

    def finish(v):
        if act == "relu2":
            r = jnp.maximum(v, 0.0)
            v = r * r
        if has_res:
            v = res_ref[...] + gt_ref[0] * v
        o_ref[...] = v.astype(o_ref.dtype)

    part = jnp.dot(x_ref[...], w_ref[...], preferred_element_type=F32)
    if nk == 1:
        finish(part)
    else:
        acc_ref, = scratch
        k = pl.program_id(2)

        @pl.when(k == 0)
        def _():
            acc_ref[...] = part

        @pl.when(k > 0)
        def _():
            acc_ref[...] += part

        @pl.when(k == nk - 1)
        def _():
            finish(acc_ref[...])


def _matmul(x, w, *, tm, tn, tk=None, out_dtype, act=None, res=None, mod=None, gate_chunk=None, name):
    m, kdim = x.shape
    n = w.shape[1]
    tk = kdim if tk is None else tk
    nk = kdim // tk
    has_res = res is not None
    in_specs = [pl.BlockSpec((tm, tk), lambda i, j, k: (i, k)),
                pl.BlockSpec((tk, tn), lambda i, j, k: (k, j))]
    args = [x, w]
    if has_res:
        nj = n // tn
        in_specs += [pl.BlockSpec((tm, tn), lambda i, j, k: (i, j)),
                     pl.BlockSpec((1, 1, tn), lambda i, j, k: (_mod_row(i, tm), 0, gate_chunk * nj + j))]
        args += [res, mod]
    return pl.pallas_call(
        functools.partial(_mm_kernel, nk=nk, act=act, has_res=has_res),
        grid=(m // tm, n // tn, nk),
        in_specs=in_specs,
        out_specs=pl.BlockSpec((tm, tn), lambda i, j, k: (i, j)),
        out_shape=jax.ShapeDtypeStruct((m, n), out_dtype),
        scratch_shapes=[pltpu.VMEM((tm, tn), F32)] if nk > 1 else [],
        compiler_params=_cparams("parallel", "parallel", "arbitrary"),
        name=name,
    )(*args)


def _mm_ws_kernel(*refs, act, has_res):
    if has_res:
        x_ref, w_ref, res_ref, gt_ref, o_ref, wbf_ref = refs
    else:
        x_ref, w_ref, o_ref, wbf_ref = refs

    @pl.when(pl.program_id(1) == 0)
    def _():
        wbf_ref[...] = w_ref[...].astype(BF16)

    v = jnp.dot(x_ref[...], wbf_ref[...], preferred_element_type=F32)
    if act == "relu2":
        r = jnp.maximum(v, 0.0)
        v = r * r
    if has_res:
        v = res_ref[...] + gt_ref[0] * v
    o_ref[...] = v.astype(o_ref.dtype)


def _matmul_ws(x, w, col0, n, *, tm, tn, out_dtype, act=None, res=None, mod=None, gate_chunk=None, name):
    m, kdim = x.shape
    has_res = res is not None
    j0 = col0 // tn
    nj = n // tn
    in_specs = [pl.BlockSpec((tm, kdim), lambda j, i: (i, 0)),
                pl.BlockSpec((kdim, tn), lambda j, i: (0, j0 + j))]
    args = [x, w]
    if has_res:
        in_specs += [pl.BlockSpec((tm, tn), lambda j, i: (i, j)),
                     pl.BlockSpec((1, 1, tn), lambda j, i: (_mod_row(i, tm), 0, gate_chunk * nj + j))]
        args += [res, mod]
    return pl.pallas_call(
        functools.partial(_mm_ws_kernel, act=act, has_res=has_res),
        grid=(nj, m // tm),
        in_specs=in_specs,
        out_specs=pl.BlockSpec((tm, tn), lambda j, i: (i, j)),
        out_shape=jax.ShapeDtypeStruct((m, n), out_dtype),
        scratch_shapes=[pltpu.VMEM((kdim, tn), BF16)],
        compiler_params=_cparams("parallel", "arbitrary"),
        name=name,
    )(*args)


def _merge_kernel(o1_ref, o2_ref, o3_ref, w1_ref, w2_ref, w3_ref, g1_ref, g2_ref, g3_ref, out_ref):
    acc = jax.nn.sigmoid(g1_ref[...]) * jnp.dot(o1_ref[...], w1_ref[...], preferred_element_type=F32)
    acc += jax.nn.sigmoid(g2_ref[...]) * jnp.dot(o2_ref[...], w2_ref[...], preferred_element_type=F32)
    acc += jax.nn.sigmoid(g3_ref[...]) * jnp.dot(o3_ref[...], w3_ref[...], preferred_element_type=F32)
    out_ref[...] = acc.astype(out_ref.dtype)


def _merge(o_ssd, o_na, o_s5, w1, w2, w3, gates):
    tm, tn = 512, 512
    nj = D_MODEL // tn
    return pl.pallas_call(
        _merge_kernel,
        grid=(M_ALL // tm, nj),
        in_specs=[pl.BlockSpec((tm, D_SSD), lambda i, j: (i, 0)),
                  pl.BlockSpec((tm, D_NA), lambda i, j: (i, 0)),
                  pl.BlockSpec((tm, D_S5), lambda i, j: (i, 0)),
                  pl.BlockSpec((D_SSD, tn), lambda i, j: (0, j)),
                  pl.BlockSpec((D_NA, tn), lambda i, j: (0, j)),
                  pl.BlockSpec((D_S5, tn), lambda i, j: (0, j)),
                  pl.BlockSpec((tm, tn), lambda i, j: (i, j)),
                  pl.BlockSpec((tm, tn), lambda i, j: (i, nj + j)),
                  pl.BlockSpec((tm, tn), lambda i, j: (i, 2 * nj + j))],
        out_specs=pl.BlockSpec((tm, tn), lambda i, j: (i, j)),
        out_shape=jax.ShapeDtypeStruct((M_ALL, D_MODEL), BF16),
        compiler_params=_cparams("parallel", "parallel"),
        name="merge",
    )(o_ssd, o_na, o_s5, w1, w2, w3, gates, gates, gates)


CONV_TM = SEQ
CONV_HALO = SUBLANES


def _conv_kernel(prev_ref, x_ref, next_ref, w_ref, b_ref, o_ref):
    _, first, last, _ = _seq_pos(pl.program_id(0), CONV_TM)
    prev = jnp.where(first, 0.0, prev_ref[...])
    nxt = jnp.where(last, 0.0, next_ref[...])
    ext = jnp.concatenate([prev, x_ref[...], nxt], axis=0)
    acc = b_ref[...]
    for k in range(SSD_CONV_W):
        off = CONV_HALO - SSD_CONV_W // 2 + k
        acc = acc + w_ref[k:k + 1, :] * ext[off:off + CONV_TM]
    o_ref[...] = _silu(acc)


def _conv_silu(xbc, w, b, col0, width):
    ct = 512
    joff = col0 // ct
    halo_per = CONV_TM // CONV_HALO
    n_halo = M_ALL // CONV_HALO
    return pl.pallas_call(
        _conv_kernel,
        grid=(M_ALL // CONV_TM, width // ct),
        in_specs=[pl.BlockSpec((CONV_HALO, ct), lambda i, j: (jnp.maximum(i * halo_per - 1, 0), joff + j)),
                  pl.BlockSpec((CONV_TM, ct), lambda i, j: (i, joff + j)),
                  pl.BlockSpec((CONV_HALO, ct), lambda i, j: (jnp.minimum((i + 1) * halo_per, n_halo - 1), joff + j)),
                  pl.BlockSpec((SSD_CONV_W, ct), lambda i, j: (0, joff + j)),
                  pl.BlockSpec((1, ct), lambda i, j: (0, joff + j))],
        out_specs=pl.BlockSpec((CONV_TM, ct), lambda i, j: (i, j)),
        out_shape=jax.ShapeDtypeStruct((M_ALL, width), F32),
        compiler_params=_cparams("parallel", "parallel"),
        name="ssd_conv",
    )(xbc, xbc, xbc, w, b.reshape(1, -1))


SSD_NCH = M_ALL // SSD_CHUNK
SSD_PAIRS = SSD_HEADS // 2
HEADS_PER_GROUP = SSD_HEADS // SSD_GROUPS


def _cumsum_rows(a):
    t = a.shape[0]
    row = lax.broadcasted_iota(jnp.int32, (t, t), 0)
    col = lax.broadcasted_iota(jnp.int32, (t, t), 1)
    tri = (col <= row).astype(BF16)
    hi = a.astype(BF16)
    r1 = a - hi.astype(F32)
    mid = r1.astype(BF16)
    lo = (r1 - mid.astype(F32)).astype(BF16)
    return (jnp.dot(tri, hi, preferred_element_type=F32) + jnp.dot(tri, mid, preferred_element_type=F32)
            + jnp.dot(tri, lo, preferred_element_type=F32))


def _ssd_kernel(x_ref, b_ref, c_ref, dt_ref, dtb_ref, alog_ref, h0_ref, y_ref, fin_ref, st_ref, aux_ref, *, d):
    T = SSD_CHUNK
    i = pl.program_id(0)
    r = i if d == 0 else SSD_NCH - 1 - i
    is_lat, seq_first, seq_last, _ = _seq_pos(r, T)
    first = seq_first if d == 0 else seq_last
    last = seq_last if d == 0 else seq_first

    @pl.when(first & is_lat)
    def _():
        st_ref[...] = h0_ref[0]

    @pl.when(first & jnp.logical_not(is_lat))
    def _():
        st_ref[...] = jnp.zeros_like(st_ref)

    dt = jax.nn.softplus(dt_ref[...] + dtb_ref[...])
    a = dt * (-jnp.exp(alog_ref[...]))
    cs = _cumsum_rows(a)
    tot = cs[T - 1:T, :]
    if d == 0:
        pot = cs
        wexp = jnp.exp(tot - cs)
    else:
        ecs = cs - a
        pot = -ecs
        wexp = jnp.exp(ecs)
    aux_ref[0] = pot.T
    aux_ref[1] = dt.T
    aux_ref[2] = (dt * wexp).T
    aux_ref[3] = jnp.broadcast_to(tot, (T, T)).T

    row = lax.broadcasted_iota(jnp.int32, (T, T), 0)
    col = lax.broadcasted_iota(jnp.int32, (T, T), 1)
    mask = (row >= col) if d == 0 else (col >= row)
    lane_lo = lax.broadcasted_iota(jnp.int32, (T, LANES), 1) < SSD_HEAD_DIM

    for g in range(SSD_GROUPS):
        gs = slice(g * SSD_STATE, (g + 1) * SSD_STATE)
        bg = b_ref[:, gs]
        cg = c_ref[:, gs]
        cb = lax.dot_general(cg.astype(BF16), bg.astype(BF16), (((1,), (1,)), ((), ())),
                             preferred_element_type=F32)
        bgt = bg.T
        for pp in range(HEADS_PER_GROUP // 2):
            pair = g * (HEADS_PER_GROUP // 2) + pp
            ps = slice(pair * LANES, (pair + 1) * LANES)
            xp = x_ref[:, ps].astype(BF16)
            st = st_ref[pair]
            st16 = st.astype(BF16)
            ys, incs, decs = [], [], []
            for e in range(2):
                h = d * SSD_HEADS + pair * 2 + e
                colb = jnp.broadcast_to(pot[:, h:h + 1], (T, T))
                lm = jnp.exp(jnp.where(mask, colb - aux_ref[0, h:h + 1, :], NEG_INF))
                mh = (cb * lm * aux_ref[1, h:h + 1, :]).astype(BF16)
                totrow = aux_ref[3, h:h + 1, :]
                oh = jnp.exp(colb) if d == 0 else jnp.exp(colb + totrow)
                co = (cg * oh).astype(BF16)
                ys.append(jnp.dot(mh, xp, preferred_element_type=F32)
                          + jnp.dot(co, st16, preferred_element_type=F32))
                btw = (bgt * aux_ref[2, h:h + 1, :]).astype(BF16)
                incs.append(jnp.dot(btw, xp, preferred_element_type=F32))
                decs.append(jnp.exp(totrow))
            y_ref[:, ps] = jnp.where(lane_lo, ys[0], ys[1])
            dec = jnp.where(lane_lo[:1], decs[0], decs[1])
            st_ref[pair] = dec * st + jnp.where(lane_lo, incs[0], incs[1])

    @pl.when(last)
    def _():
        fin_ref[0] = st_ref[...]


def _ssd_scan(xs, bm, cm, dt_raw, dt_bias, a_log, h0, d):
    T = SSD_CHUNK

    def rblk(i):
        return i if d == 0 else SSD_NCH - 1 - i

    def seq_of(i):
        return _seq_pos(rblk(i), T)[3]

    return pl.pallas_call(
        functools.partial(_ssd_kernel, d=d),
        grid=(SSD_NCH,),
        in_specs=[pl.BlockSpec((T, D_SSD), lambda i: (rblk(i), 0)),
                  pl.BlockSpec((T, SSD_GN), lambda i: (rblk(i), 0)),
                  pl.BlockSpec((T, SSD_GN), lambda i: (rblk(i), 0)),
                  pl.BlockSpec((T, 2 * SSD_HEADS), lambda i: (rblk(i), 0)),
                  pl.BlockSpec((1, 2 * SSD_HEADS), lambda i: (0, 0)),
                  pl.BlockSpec((1, 2 * SSD_HEADS), lambda i: (0, 0)),
                  pl.BlockSpec((1, SSD_PAIRS, SSD_STATE, LANES),
                               lambda i: (jnp.maximum(seq_of(i) - BATCH, 0), 0, 0, 0))],
        out_specs=[pl.BlockSpec((T, D_SSD), lambda i: (rblk(i), 0)),
                   pl.BlockSpec((1, SSD_PAIRS, SSD_STATE, LANES), lambda i: (seq_of(i), 0, 0, 0))],
        out_shape=[jax.ShapeDtypeStruct((M_ALL, D_SSD), F32),
                   jax.ShapeDtypeStruct((N_SEQ_ALL, SSD_PAIRS, SSD_STATE, LANES), F32)],
        scratch_shapes=[pltpu.VMEM((SSD_PAIRS, SSD_STATE, LANES), F32),
                        pltpu.VMEM((4, T, T), F32)],
        compiler_params=_cparams("arbitrary"),
        name=f"ssd_scan{d}",
    )(xs, bm, cm, dt_raw, dt_bias.reshape(1, -1), a_log.reshape(1, -1), h0)


def _ssd_out_kernel(yf_ref, yb_ref, xs_ref, z_ref, d_ref, g_ref, o_ref):
    y = (yf_ref[...] + yb_ref[...] + d_ref[...] * xs_ref[...]) * _silu(z_ref[...])
    gw = D_SSD // SSD_GROUPS
    for g in range(SSD_GROUPS):
        gs = slice(g * gw, (g + 1) * gw)
        yg = y[:, gs]
        ms = jnp.mean(yg * yg, axis=-1, keepdims=True)
        o_ref[:, gs] = (yg * lax.rsqrt(ms + EPS) * g_ref[:, gs]).astype(o_ref.dtype)


def _ssd_out(yf, yb, xs, z, d_skip, norm_g):
    tm = 256
    row = pl.BlockSpec((tm, D_SSD), lambda i: (i, 0))
    vec = pl.BlockSpec((1, D_SSD), lambda i: (0, 0))
    return pl.pallas_call(
        _ssd_out_kernel,
        grid=(M_ALL // tm,),
        in_specs=[row, row, row, row, vec, vec],
        out_specs=row,
        out_shape=jax.ShapeDtypeStruct((M_ALL, D_SSD), BF16),
        compiler_params=_cparams("parallel"),
        name="ssd_out",
    )(yf, yb, xs, z, jnp.repeat(d_skip, SSD_HEAD_DIM).reshape(1, D_SSD), norm_g.reshape(1, D_SSD))


def _ssd_state_to_pairs(s):
    b = s.shape[0]
    s = s.reshape(b, SSD_PAIRS, 2, SSD_HEAD_DIM, SSD_STATE)
    return s.transpose(0, 1, 4, 2, 3).reshape(b, SSD_PAIRS, SSD_STATE, LANES)


def _ssd_state_from_pairs(s):
    b = s.shape[0]
    s = s.reshape(b, SSD_PAIRS, SSD_STATE, 2, SSD_HEAD_DIM)
    return s.transpose(0, 1, 3, 4, 2).reshape(b, SSD_HEADS, SSD_HEAD_DIM, SSD_STATE)


def _head_rms(x, g):
    return x * lax.rsqrt(jnp.mean(x * x, axis=-1, keepdims=True) + EPS) * g


ATT_SCALE = NA_HEAD_DIM ** -0.5
CTX_HEADS_PER_STEP = 4


def _ctx_attn_kernel(q_ref, k_ref, v_ref, qg_ref, kg_ref, o_ref, kn_ref):
    for hh in range(CTX_HEADS_PER_STEP):
        hs = slice(hh * NA_HEAD_DIM, (hh + 1) * NA_HEAD_DIM)
        qn = _head_rms(q_ref[:, hs], qg_ref[...])
        kn = _head_rms(k_ref[:, hs], kg_ref[...])
        kn_ref[:, hs] = kn
        s = lax.dot_general(qn.astype(BF16), kn.astype(BF16), (((1,), (1,)), ((), ())),
                            preferred_element_type=F32) * ATT_SCALE
        p = jnp.exp(s - jnp.max(s, axis=-1, keepdims=True))
        den = jnp.sum(p, axis=-1, keepdims=True)
        o = jnp.dot(p.astype(BF16), v_ref[:, hs].astype(BF16), preferred_element_type=F32)
        o_ref[:, hs] = (o / den).astype(o_ref.dtype)


def _ctx_attn(q, k, v, q_g, k_g):
    w = CTX_HEADS_PER_STEP * NA_HEAD_DIM
    blk = pl.BlockSpec((SEQ, w), lambda b, j: (b, j))
    vec = pl.BlockSpec((1, NA_HEAD_DIM), lambda b, j: (0, 0))
    return pl.pallas_call(
        _ctx_attn_kernel,
        grid=(BATCH, D_NA // w),
        in_specs=[blk, blk, blk, vec, vec],
        out_specs=[blk, blk],
        out_shape=[jax.ShapeDtypeStruct((M_CTX, D_NA), BF16), jax.ShapeDtypeStruct((M_CTX, D_NA), F32)],
        compiler_params=_cparams("parallel", "parallel"),
        name="ctx_attn",
    )(q, k, v, q_g.reshape(1, -1), k_g.reshape(1, -1))


NA_BAND_ROWS = 8
NA_BANDS = GRID_ROWS // NA_BAND_ROWS
NA_KEY_ROWS = 2 * NA_BAND_ROWS
NA_BAND_Q = NA_BAND_ROWS * GRID_W
NA_BAND_K = NA_KEY_ROWS * GRID_W


def _na_band_key_row0(band):
    return np.clip(band * NA_BAND_ROWS - NA_WIN_R // 2, 0, GRID_ROWS - NA_KEY_ROWS)


def _na_bias_pairs(rpb):
    qc = np.arange(GRID_W)[:, None]
    kc = np.arange(GRID_W)[None, :]
    ws = np.clip(qc - NA_WIN_C // 2, 0, GRID_W - NA_WIN_C)
    col_ok = (kc >= ws) & (kc < ws + NA_WIN_C)
    dc = np.clip(kc - qc + NA_WIN_C - 1, 0, 2 * NA_WIN_C - 2)
    tm = jnp.where(col_ok, rpb[:, :, dc], NEG_INF)
    neg = jnp.full((NA_HEADS, 1, GRID_W, GRID_W), NEG_INF, F32)
    tm = jnp.concatenate([neg, tm, neg], axis=1)
    return jnp.concatenate([tm[:, :-1], tm[:, 1:]], axis=-1)


def _na_fill_bias(bias_ref, tp_ref, band):
    row0 = int(_na_band_key_row0(band))
    lane_lo = lax.broadcasted_iota(jnp.int32, (GRID_W, 2 * GRID_W), 1) < GRID_W
    for qr in range(NA_BAND_ROWS):
        r = band * NA_BAND_ROWS + qr
        rs = min(max(r - NA_WIN_R // 2, 0), GRID_ROWS - NA_WIN_R)
        for m in range(NA_KEY_ROWS // 2):
            kr = row0 + 2 * m
            ok_l = rs <= kr < rs + NA_WIN_R
            ok_r = rs <= kr + 1 < rs + NA_WIN_R
            dst = (slice(qr * GRID_W, (qr + 1) * GRID_W), slice(m * 2 * GRID_W, (m + 1) * 2 * GRID_W))
            if not (ok_l or ok_r):
                bias_ref[dst] = jnp.full((GRID_W, 2 * GRID_W), NEG_INF, F32)
                continue
            blk = tp_ref[0, kr - r + NA_WIN_R]
            if not ok_r:
                blk = jnp.where(lane_lo, blk, NEG_INF)
            elif not ok_l:
                blk = jnp.where(lane_lo, NEG_INF, blk)
            bias_ref[dst] = blk


def _lat_attn_kernel(q_ref, k_ref, v_ref, kc_ref, vc_ref, tp_ref, qg_ref, kg_ref, o_ref, bias_ref):
    band = pl.program_id(1)
    for n in range(NA_BANDS):
        @pl.when((pl.program_id(2) == 0) & (band == n))
        def _(n=n):
            _na_fill_bias(bias_ref, tp_ref, n)

    row0 = jnp.clip(band * NA_BAND_ROWS - NA_WIN_R // 2, 0, GRID_ROWS - NA_KEY_ROWS)
    k0 = pl.multiple_of(row0 * GRID_W, NA_WIN_R // 2 * GRID_W)
    qn = _head_rms(q_ref[...], qg_ref[...]).astype(BF16)
    kn = _head_rms(k_ref[pl.ds(k0, NA_BAND_K), :], kg_ref[...]).astype(BF16)
    vw = v_ref[pl.ds(k0, NA_BAND_K), :].astype(BF16)
    nt = (((1,), (1,)), ((), ()))
    s_loc = lax.dot_general(qn, kn, nt, preferred_element_type=F32) * ATT_SCALE + bias_ref[...]
    s_ctx = lax.dot_general(qn, kc_ref[0].astype(BF16), nt, preferred_element_type=F32) * ATT_SCALE
    m = jnp.maximum(jnp.max(s_loc, axis=-1, keepdims=True), jnp.max(s_ctx, axis=-1, keepdims=True))
    p_loc = jnp.exp(s_loc - m)
    p_ctx = jnp.exp(s_ctx - m)
    den = jnp.sum(p_loc, axis=-1, keepdims=True) + jnp.sum(p_ctx, axis=-1, keepdims=True)
    o = (jnp.dot(p_loc.astype(BF16), vw, preferred_element_type=F32)
         + jnp.dot(p_ctx.astype(BF16), vc_ref[0].astype(BF16), preferred_element_type=F32))
    o_ref[...] = (o / den).astype(o_ref.dtype)


def _lat_attn(q, k, v, k_ctx, v_ctx, bias_pairs, q_g, k_g):
    q_blk0 = M_CTX // NA_BAND_Q
    seq_blk0 = M_CTX // DEC_SEQ
    hd = NA_HEAD_DIM
    vec = pl.BlockSpec((1, hd), lambda h, n, b: (0, 0))
    return pl.pallas_call(
        _lat_attn_kernel,
        grid=(NA_HEADS, NA_BANDS, DEC_BATCH),
        in_specs=[pl.BlockSpec((NA_BAND_Q, hd), lambda h, n, b: (q_blk0 + b * NA_BANDS + n, h)),
                  pl.BlockSpec((DEC_SEQ, hd), lambda h, n, b: (seq_blk0 + b, h)),
                  pl.BlockSpec((DEC_SEQ, hd), lambda h, n, b: (seq_blk0 + b, h)),
                  pl.BlockSpec((1, PAST_LEN, hd), lambda h, n, b: (b, 0, h)),
                  pl.BlockSpec((1, PAST_LEN, hd), lambda h, n, b: (b, 0, h)),
                  pl.BlockSpec((1, 2 * NA_WIN_R, GRID_W, 2 * GRID_W), lambda h, n, b: (h, 0, 0, 0)),
                  vec, vec],
        out_specs=pl.BlockSpec((NA_BAND_Q, hd), lambda h, n, b: (b * NA_BANDS + n, h)),
        out_shape=jax.ShapeDtypeStruct((M_LAT, D_NA), BF16),
        scratch_shapes=[pltpu.VMEM((NA_BAND_Q, NA_BAND_K), F32)],
        compiler_params=_cparams("parallel", "parallel", "arbitrary"),
        name="lat_attn",
    )(q, k, v, k_ctx, v_ctx, bias_pairs, q_g.reshape(1, -1), k_g.reshape(1, -1))


S5_SEQS = 4
S5_ROWS = 2 * S5_SEQS
S5_GB = LANES // S5_CH
S5_NBLK = D_S5 // LANES
S5_SW = S5_GB * S5_STATE
S5_CB = S5_SW // LANES


def _s5_kernel(*refs, has_h0, tsteps):
    if has_h0:
        (u_ref, b_ref, c_ref, are_ref, aim_ref, h0re_ref, h0im_ref,
         y_ref, yrev_ref, fre_ref, fim_ref, bu_ref, hre_ref, him_ref) = refs
    else:
        u_ref, b_ref, c_ref, are_ref, aim_ref, y_ref, yrev_ref, fre_ref, fim_ref, bu_ref, hre_ref, him_ref = refs
    T = tsteps
    rows = T * S5_ROWS

    @pl.when(pl.program_id(2) == 0)
    def _():
        if has_h0:
            hre_ref[...] = h0re_ref[0]
            him_ref[...] = h0im_ref[0]
        else:
            hre_ref[...] = jnp.zeros_like(hre_ref)
            him_ref[...] = jnp.zeros_like(him_ref)

    is_fwd = lax.broadcasted_iota(jnp.int32, (rows, LANES), 0) % S5_ROWS < S5_SEQS
    u = u_ref[0].reshape(rows, LANES)
    lhs = jnp.concatenate([jnp.where(is_fwd, u, 0.0), jnp.where(is_fwd, 0.0, u)], axis=1).astype(BF16)
    bu_ref[...] = jnp.dot(lhs, b_ref[0], preferred_element_type=F32)

    cols = [slice(cb * LANES, (cb + 1) * LANES) for cb in range(S5_CB)]
    a_re = [are_ref[0, :, cs] for cs in cols]
    a_im = [aim_ref[0, :, cs] for cs in cols]

    def step(t, carry):
        h_re, h_im = carry
        rs = pl.ds(pl.multiple_of(t * S5_ROWS, S5_ROWS), S5_ROWS)
        n_re, n_im = [], []
        for cb in range(S5_CB):
            cre = cols[cb]
            cim = slice(S5_SW + cb * LANES, S5_SW + (cb + 1) * LANES)
            r = a_re[cb] * h_re[cb] - a_im[cb] * h_im[cb] + bu_ref[rs, cre]
            m = a_re[cb] * h_im[cb] + a_im[cb] * h_re[cb] + bu_ref[rs, cim]
            bu_ref[rs, cre] = r
            bu_ref[rs, cim] = m
            n_re.append(r)
            n_im.append(m)
        return tuple(n_re), tuple(n_im)

    init = (tuple(hre_ref[:, cs] for cs in cols), tuple(him_ref[:, cs] for cs in cols))
    h_re, h_im = lax.fori_loop(0, T, step, init, unroll=4)
    for cb in range(S5_CB):
        hre_ref[:, cols[cb]] = h_re[cb]
        him_ref[:, cols[cb]] = h_im[cb]
        fre_ref[0, 0, :, cols[cb]] = h_re[cb]
        fim_ref[0, 0, :, cols[cb]] = h_im[cb]

    y2 = jnp.dot(bu_ref[...].astype(BF16), c_ref[0], preferred_element_type=F32)
    y = jnp.where(is_fwd, y2[:, :LANES], y2[:, LANES:]).reshape(T, S5_ROWS, LANES)
    y_ref[0] = y
    for t in range(T):
        yrev_ref[0, T - 1 - t] = y[t]


def _s5_scan(u8, mats, h0, tsteps):
    nset, L, _, _ = u8.shape
    nt = L // tsteps
    bcat, ccat, are, aim = mats
    has_h0 = h0 is not None
    ublk = pl.BlockSpec((1, tsteps, S5_ROWS, LANES), lambda s, j, i: (s, i, 0, j))
    rowblk = pl.BlockSpec((1, S5_ROWS, S5_SW), lambda s, j, i: (j, 0, 0))
    in_specs = [ublk,
                pl.BlockSpec((1, 2 * LANES, 2 * S5_SW), lambda s, j, i: (j, 0, 0)),
                pl.BlockSpec((1, 2 * S5_SW, 2 * LANES), lambda s, j, i: (j, 0, 0)),
                rowblk, rowblk]
    args = [u8, bcat, ccat, are, aim]
    if has_h0:
        in_specs += [rowblk, rowblk]
        args += list(h0)
    fin = jax.ShapeDtypeStruct((nset, S5_NBLK, S5_ROWS, S5_SW), F32)
    fin_spec = pl.BlockSpec((1, 1, S5_ROWS, S5_SW), lambda s, j, i: (s, j, 0, 0))
    return pl.pallas_call(
        functools.partial(_s5_kernel, has_h0=has_h0, tsteps=tsteps),
        grid=(nset, S5_NBLK, nt),
        in_specs=in_specs,
        out_specs=[ublk, pl.BlockSpec((1, tsteps, S5_ROWS, LANES), lambda s, j, i: (s, nt - 1 - i, 0, j)),
                   fin_spec, fin_spec],
        out_shape=[jax.ShapeDtypeStruct(u8.shape, F32), jax.ShapeDtypeStruct(u8.shape, F32), fin, fin],
        scratch_shapes=[pltpu.VMEM((tsteps * S5_ROWS, 2 * S5_SW), F32),
                        pltpu.VMEM((S5_ROWS, S5_SW), F32), pltpu.VMEM((S5_ROWS, S5_SW), F32)],
        compiler_params=_cparams("parallel", "parallel", "arbitrary"),
        name="s5_scan_lat" if has_h0 else "s5_scan_ctx",
    )(*args)


def _s5_matrices(lam_re, lam_im, log_step, b_re, b_im, c_re, c_im):
    step = jnp.exp(log_step)[..., None]
    mag = jnp.exp(lam_re * step)
    ab_re, ab_im = mag * jnp.cos(lam_im * step), mag * jnp.sin(lam_im * step)
    den = lam_re * lam_re + lam_im * lam_im
    f_re = ((ab_re - 1) * lam_re + ab_im * lam_im) / den
    f_im = (ab_im * lam_re - (ab_re - 1) * lam_im) / den
    bb_re = f_re[..., None] * b_re - f_im[..., None] * b_im
    bb_im = f_re[..., None] * b_im + f_im[..., None] * b_re
    eye = jnp.eye(S5_GB, dtype=F32)

    def in_blocks(bb):
        bb = bb.reshape(2, S5_NBLK, S5_GB, S5_STATE, S5_CH)
        return jnp.einsum("djgpc,gh->djgchp", bb, eye).reshape(2, S5_NBLK, LANES, S5_SW)

    def out_blocks(cc):
        cc = cc.reshape(2, S5_NBLK, S5_GB, S5_CH, S5_STATE)
        return jnp.einsum("djgcp,gh->djgphc", cc, eye).reshape(2, S5_NBLK, S5_SW, LANES)

    def decay_rows(ab):
        ab = ab.reshape(2, S5_NBLK, S5_SW).transpose(1, 0, 2)
        return jnp.repeat(ab, S5_SEQS, axis=1)

    bcat = jnp.concatenate([in_blocks(bb_re), in_blocks(bb_im)], axis=-1)
    bcat = bcat.transpose(1, 0, 2, 3).reshape(S5_NBLK, 2 * LANES, 2 * S5_SW).astype(BF16)
    ccat = jnp.concatenate([out_blocks(c_re), -out_blocks(c_im)], axis=-2)
    ccat = ccat.transpose(1, 2, 0, 3).reshape(S5_NBLK, 2 * S5_SW, 2 * LANES).astype(BF16)
    return bcat, ccat, decay_rows(ab_re), decay_rows(ab_im)


def _s5_glu_kernel(u_ref, yf_ref, yb_ref, d_ref, w_ref, b_ref, o_ref):
    y = jax.nn.gelu(d_ref[...] * u_ref[...] + yf_ref[...] + yb_ref[...])
    gate = jnp.dot(y.astype(BF16), w_ref[...], preferred_element_type=F32) + b_ref[...]
    o_ref[...] = (y * jax.nn.sigmoid(gate)).astype(o_ref.dtype)


def _s5_glu(u, yf, yb, d_skip, glu_w, glu_b):
    tm = 256
    row = pl.BlockSpec((tm, D_S5), lambda i: (i, 0))
    vec = pl.BlockSpec((1, D_S5), lambda i: (0, 0))
    return pl.pallas_call(
        _s5_glu_kernel,
        grid=(M_ALL // tm,),
        in_specs=[row, row, row, vec, pl.BlockSpec((D_S5, D_S5), lambda i: (0, 0)), vec],
        out_specs=row,
        out_shape=jax.ShapeDtypeStruct((M_ALL, D_S5), BF16),
        compiler_params=_cparams("parallel"),
        name="s5_glu",
    )(u, yf, yb, d_skip.reshape(1, D_S5), glu_w, glu_b.reshape(1, D_S5))


def _mixer_ssd(z, xbc, dt_raw, p, h0_ssd):
    w, b = p["ssd_conv_w"], p["ssd_conv_b"]
    xs = _conv_silu(xbc, w, b, 0, D_SSD)
    bm = _conv_silu(xbc, w, b, D_SSD, SSD_GN)
    cm = _conv_silu(xbc, w, b, D_SSD + SSD_GN, SSD_GN)
    ys, fins = [], []
    for d in range(2):
        y, fin = _ssd_scan(xs, bm, cm, dt_raw, p["ssd_dt_bias"], p["ssd_a_log"],
                           _ssd_state_to_pairs(h0_ssd[:, d]), d)
        ys.append(y)
        fins.append(_ssd_state_from_pairs(fin[:BATCH]))
    o = _ssd_out(ys[0], ys[1], xs, z, p["ssd_d"], p["ssd_norm_g"])
    return o, jnp.stack(fins, axis=1)


def _mixer_attn(q, k, v, p, k_ctx, v_ctx):
    o_c, kn_c = _ctx_attn(q, k, v, p["na_q_g"], p["na_k_g"])
    o_l = _lat_attn(q, k, v, k_ctx.reshape(DEC_BATCH, PAST_LEN, D_NA), v_ctx.reshape(DEC_BATCH, PAST_LEN, D_NA),
                    _na_bias_pairs(p["na_rpb"]), p["na_q_g"], p["na_k_g"])
    return jnp.concatenate([o_c, o_l], axis=0), kn_c


def _mixer_s5(u, p, h0_s5):
    mats = _s5_matrices(p["s5_lam_re"], p["s5_lam_im"], p["s5_log_step"], p["s5_b_re"], p["s5_b_im"],
                        p["s5_c_re"], p["s5_c_im"])

    def time_major(part, nset, L):
        fwd = part.reshape(nset, S5_SEQS, L, D_S5).transpose(0, 2, 1, 3)
        return jnp.concatenate([fwd, jnp.flip(fwd, axis=1)], axis=2)

    def seq_major(y8, rows):
        nset, L = y8.shape[:2]
        return y8[:, :, rows].transpose(0, 2, 1, 3).reshape(nset * S5_SEQS * L, D_S5)

    def h0_rows(part):
        part = part.transpose(1, 0, 2, 3).reshape(S5_ROWS, S5_NBLK, S5_SW)
        return part.transpose(1, 0, 2)

    y_c, yrev_c, fre, fim = _s5_scan(time_major(u[:M_CTX], BATCH // S5_SEQS, SEQ), mats, None, SEQ)
    y_l, yrev_l, _, _ = _s5_scan(time_major(u[M_CTX:], DEC_BATCH // S5_SEQS, DEC_SEQ), mats,
                                 (h0_rows(h0_s5[:, :, 0]), h0_rows(h0_s5[:, :, 1])), SEQ)
    fwd_rows, bwd_rows = slice(0, S5_SEQS), slice(S5_SEQS, S5_ROWS)
    yf_c, yb_c = seq_major(y_c, fwd_rows), seq_major(yrev_c, bwd_rows)
    yf_l, yb_l = seq_major(y_l, fwd_rows), seq_major(yrev_l, bwd_rows)

    def fin_states(f):
        f = f.reshape(BATCH // S5_SEQS, S5_NBLK, 2, S5_SEQS, S5_SW).transpose(0, 3, 2, 1, 4)
        return f.reshape(BATCH, 2, S5_GROUPS, S5_STATE)

    o = _s5_glu(u, jnp.concatenate([yf_c, yf_l], axis=0), jnp.concatenate([yb_c, yb_l], axis=0),
                p["s5_d"], p["s5_glu_w"].astype(BF16), p["s5_glu_b"])
    return o, jnp.stack([fin_states(fre), fin_states(fim)], axis=2)


def _layer(x, mod, p, ctx_l):
    bf = lambda w: w.astype(BF16)
    cuts = np.cumsum((0, 3 * D_MODEL, D_SSD, SSD_CONV_CH, 2 * SSD_HEADS, D_NA, D_NA, D_NA, D_S5))
    w_in = p["w_in"]
    h = _norm_mod(x, p["norm1_g"], mod, shift_chunk=0, scale_chunk=1)

    def proj(idx, tn):
        w = bf(w_in[:, cuts[idx]:cuts[idx + 1]])
        return _matmul(h, w, tm=1024, tn=tn, out_dtype=F32, name=f"in_proj{idx}")

    def proj_ws(idx):
        return _matmul_ws(h, w_in, int(cuts[idx]), int(cuts[idx + 1] - cuts[idx]), tm=1024, tn=512, out_dtype=F32,
                          name=f"in_proj{idx}")

    gates = proj_ws(0)
    z = proj_ws(1)
    xbc = proj_ws(2)
    dt_raw = proj(3, 128)
    q = proj(4, 1024)
    k = proj(5, 1024)
    v = proj(6, 1024)
    u = proj(7, 1024)

    k_ctx, v_ctx, h0_ssd, h0_s5 = ctx_l
    o_ssd, st_ssd = _mixer_ssd(z, xbc, dt_raw, p, h0_ssd)
    o_na, kn_c = _mixer_attn(q, k, v, p, k_ctx, v_ctx)
    o_s5, st_s5 = _mixer_s5(u, p, h0_s5)

    merged = _merge(o_ssd, o_na, o_s5, bf(p["w_po_ssd"]), bf(p["w_po_na"]), bf(p["w_po_s5"]), gates)
    x = _matmul_ws(merged, p["w_o"], 0, D_MODEL, tm=1024, tn=512, out_dtype=F32, res=x, mod=mod, gate_chunk=2,
                   name="w_o")
    h2 = _norm_mod(x, p["norm2_g"], mod, shift_chunk=3, scale_chunk=4)
    a = _matmul_ws(h2, p["mlp_w1"], 0, D_FF, tm=1024, tn=512, out_dtype=BF16, act="relu2", name="mlp1")
    x = _matmul(a, bf(p["mlp_w2"]), tm=1024, tn=1024, tk=2048, out_dtype=F32, res=x, mod=mod, gate_chunk=5,
                name="mlp2")
    cache_shape = (BATCH, SEQ, NA_HEADS, NA_HEAD_DIM)
    return x, (kn_c.reshape(cache_shape), v[:M_CTX].reshape(cache_shape), st_ssd, st_s5)


def kernel(x_prompt, x_sample, c, cache_k, cache_v, state_ssd, state_s5, c_ctx, ada_w, ada_b, norm1_g, norm2_g, w_in, ssd_conv_w, ssd_conv_b, ssd_dt_bias, ssd_a_log, ssd_d, ssd_norm_g, na_q_g, na_k_g, na_rpb, s5_lam_re, s5_lam_im, s5_log_step, s5_b_re, s5_b_im, s5_c_re, s5_c_im, s5_d, s5_glu_w, s5_glu_b, w_po_ssd, w_po_na, w_po_s5, w_o, mlp_w1, mlp_w2):
    params = dict(norm1_g=norm1_g, norm2_g=norm2_g, w_in=w_in, ssd_conv_w=ssd_conv_w, ssd_conv_b=ssd_conv_b,
                  ssd_dt_bias=ssd_dt_bias, ssd_a_log=ssd_a_log, ssd_d=ssd_d, ssd_norm_g=ssd_norm_g,
                  na_q_g=na_q_g, na_k_g=na_k_g, na_rpb=na_rpb, s5_lam_re=s5_lam_re, s5_lam_im=s5_lam_im,
                  s5_log_step=s5_log_step, s5_b_re=s5_b_re, s5_b_im=s5_b_im, s5_c_re=s5_c_re, s5_c_im=s5_c_im,
                  s5_d=s5_d, s5_glu_w=s5_glu_w, s5_glu_b=s5_glu_b, w_po_ssd=w_po_ssd, w_po_na=w_po_na,
                  w_po_s5=w_po_s5, w_o=w_o, mlp_w1=mlp_w1, mlp_w2=mlp_w2)
    cvec = jnp.concatenate([c_ctx[None, :], c, jnp.zeros((MOD_ROWS - 1 - DEC_BATCH, D_MODEL), F32)], axis=0)
    mods = _ada_mod(cvec, ada_w, ada_b)
    x = jnp.concatenate([x_prompt.reshape(M_CTX, D_MODEL), x_sample.reshape(M_LAT, D_MODEL)], axis=0)
    ks, vs, ssds, s5s = [], [], [], []
    for l in range(DEPTH):
        p = {name: val[l] for name, val in params.items()}
        mod = mods[l].reshape(MOD_ROWS, 1, N_MOD * D_MODEL)
        ctx_l = (cache_k[:, l], cache_v[:, l], state_ssd[:, l], state_s5[:, l])
        x, (k_l, v_l, ssd_l, s5_l) = _layer(x, mod, p, ctx_l)
        ks.append(k_l)
        vs.append(v_l)
        ssds.append(ssd_l)
        s5s.append(s5_l)
    y_prompt = x[:M_CTX].reshape(BATCH, SEQ, D_MODEL)
    y_sample = x[M_CTX:].reshape(DEC_BATCH, DEC_SEQ, D_MODEL)
    return (y_prompt, y_sample, jnp.stack(ks, axis=1), jnp.stack(vs, axis=1),
            jnp.stack(ssds, axis=1), jnp.stack(s5s, axis=1))
```
